```python
import functools
import jax, jax.numpy as jnp
from jax import lax
import numpy as np

D_MODEL = 1024
BATCH = 8
SEQ = 2048
DEPTH = 2

CHUNK = 64
Q_BLOCK = 128
MLA_HEADS = 8
MLA_NOPE = 64
MLA_ROPE = 32
MLA_V = 64
Q_LORA = 384
KV_LORA = 256
ROPE_THETA = 10000.0
SB_HEADS = 8
SB_DIM = 64
C_HEADS = 16
C_DIM = 64
LEFT_CHUNKS = 8
BAND = (LEFT_CHUNKS + 1) * CHUNK
REL_CLIP = 256
D_FF = -(-8 * D_MODEL // (3 * 256)) * 256
EVEN_IN = Q_LORA + KV_LORA + MLA_ROPE + 3 * SB_HEADS * SB_DIM
MIX_EVEN = MLA_HEADS * MLA_V + SB_HEADS * SB_DIM
MIX_ODD = C_HEADS * C_DIM
N_EVEN = (DEPTH + 1) // 2
N_ODD = DEPTH // 2
RMS_EPS = 1e-6

kernel_name = "hybrid_mla_stickbreak_chunkband_encoder"


def rms_norm(x, g):
    x32 = x.astype(jnp.float32)
    y = x32 * lax.rsqrt(jnp.mean(x32 * x32, axis=-1, keepdims=True) + RMS_EPS)
    return (y * g.astype(jnp.float32)).astype(x.dtype)


def rope_tables(seq, dim):
    pos = jnp.arange(seq, dtype=jnp.float32)
    inv_freq = ROPE_THETA ** (-jnp.arange(0, dim, 2, dtype=jnp.float32) / dim)
    ang = pos[:, None] * inv_freq[None, :]
    return jnp.cos(ang), jnp.sin(ang)


def apply_rope(x, cos, sin):
    half = x.shape[-1] // 2
    c, s = cos.astype(x.dtype), sin.astype(x.dtype)
    x1, x2 = x[..., :half], x[..., half:]
    return jnp.concatenate([x1 * c - x2 * s, x2 * c + x1 * s], axis=-1)


def swiglu(u, w_gate, w_up, w_down):
    return (jax.nn.silu(u @ w_gate) * (u @ w_up)) @ w_down


def mla_stick_breaking_mixer(u, w_in, g_cq, w_uq, g_ckv, w_ukv, w_out):
    bsz, seq, _ = u.shape
    proj = u @ w_in
    o1 = Q_LORA
    o2 = o1 + KV_LORA
    o3 = o2 + MLA_ROPE
    nb = SB_HEADS * SB_DIM
    c_q, c_kv, k_r = proj[..., :o1], proj[..., o1:o2], proj[..., o2:o3]
    q_b = proj[..., o3:o3 + nb].reshape(bsz, seq, SB_HEADS, SB_DIM)
    k_b = proj[..., o3 + nb:o3 + 2 * nb].reshape(bsz, seq, SB_HEADS, SB_DIM)
    v_b = proj[..., o3 + 2 * nb:].reshape(bsz, seq, SB_HEADS, SB_DIM)

    cos, sin = rope_tables(seq, MLA_ROPE)
    q_a = (rms_norm(c_q, g_cq) @ w_uq).reshape(bsz, seq, MLA_HEADS, MLA_NOPE + MLA_ROPE)
    q_a = jnp.concatenate([q_a[..., :MLA_NOPE],
                           apply_rope(q_a[..., MLA_NOPE:], cos[:, None, :], sin[:, None, :])], axis=-1)
    kv = (rms_norm(c_kv, g_ckv) @ w_ukv).reshape(bsz, seq, MLA_HEADS, MLA_NOPE + MLA_V)
    k_rope = apply_rope(k_r, cos, sin)
    k_a = jnp.concatenate([kv[..., :MLA_NOPE],
                           jnp.broadcast_to(k_rope[:, :, None, :], (bsz, seq, MLA_HEADS, MLA_ROPE))], axis=-1)
    v_a = kv[..., MLA_NOPE:]
    scale_a = (MLA_NOPE + MLA_ROPE) ** -0.5
    scale_b = SB_DIM ** -0.5

    outs_a, outs_b = [], []
    for blk in range(seq // Q_BLOCK):
        t0 = blk * Q_BLOCK
        kend = t0 + Q_BLOCK
        t_pos = t0 + jnp.arange(Q_BLOCK)[:, None]
        s_pos = jnp.arange(kend)[None, :]
        s_a = jnp.einsum('bqhd,bkhd->bhqk', q_a[:, t0:kend], k_a[:, :kend]).astype(jnp.float32) * scale_a
        chunk_ok = (s_pos // CHUNK) <= (t_pos // CHUNK)
        p_a = jax.nn.softmax(jnp.where(chunk_ok, s_a, -jnp.inf), axis=-1)
        outs_a.append(jnp.einsum('bhqk,bkhd->bqhd', p_a.astype(v_a.dtype), v_a[:, :kend]))
        z = jnp.einsum('bqhd,bkhd->bhqk', q_b[:, t0:kend], k_b[:, :kend]).astype(jnp.float32) * scale_b
        before = s_pos < t_pos
        log_keep = jnp.where(before, jax.nn.log_sigmoid(-z), 0.0)
        log_between = lax.cumsum(log_keep, axis=3, reverse=True) - log_keep
        w_b = jnp.where(before, jnp.exp(jax.nn.log_sigmoid(z) + log_between), 0.0)
        outs_b.append(jnp.einsum('bhqk,bkhd->bqhd', w_b.astype(v_b.dtype), v_b[:, :kend]))

    o_a = jnp.concatenate(outs_a, axis=1).reshape(bsz, seq, MLA_HEADS * MLA_V)
    o_b = jnp.concatenate(outs_b, axis=1).reshape(bsz, seq, SB_HEADS * SB_DIM)
    return jnp.concatenate([o_a, o_b], axis=-1) @ w_out


def chunk_band_mixer(u, w_qkv, rel_bias, w_out):
    bsz, seq, _ = u.shape
    n_chunks = seq // CHUNK
    qkv = (u @ w_qkv).reshape(bsz, n_chunks, CHUNK, 3, C_HEADS, C_DIM)
    q, k, v = qkv[:, :, :, 0], qkv[:, :, :, 1], qkv[:, :, :, 2]
    pad = ((0, 0), (LEFT_CHUNKS, 0), (0, 0), (0, 0), (0, 0))
    k_p = jnp.pad(k, pad)
    v_p = jnp.pad(v, pad)
    scores = jnp.concatenate(
        [jnp.einsum('bnqhd,bnkhd->bnhqk', q, k_p[:, i:i + n_chunks]) for i in range(LEFT_CHUNKS + 1)],
        axis=-1).astype(jnp.float32) * (C_DIM ** -0.5)
    q_in = jnp.arange(CHUNK)[:, None]
    j = jnp.arange(BAND)[None, :]
    rel = (LEFT_CHUNKS - j // CHUNK) * CHUNK + q_in - j % CHUNK
    bias = rel_bias[:, jnp.clip(rel, -REL_CLIP, REL_CLIP) + REL_CLIP]
    valid = (jnp.arange(n_chunks)[:, None] - LEFT_CHUNKS + j // CHUNK) >= 0
    scores = scores + bias.astype(jnp.float32)[None, None]
    p = jax.nn.softmax(jnp.where(valid[None, :, None, None, :], scores, -jnp.inf), axis=-1).astype(v.dtype)
    parts = [jnp.einsum('bnhqk,bnkhd->bnqhd', p[..., i * CHUNK:(i + 1) * CHUNK], v_p[:, i:i + n_chunks])
             for i in range(LEFT_CHUNKS + 1)]
    o = functools.reduce(jnp.add, parts)
    return o.reshape(bsz, seq, MIX_ODD) @ w_out


def setup_inputs(seed: int = 0) -> dict:
    key = jax.random.key(seed)
    ks = jax.random.split(key, 17)
    f32 = jnp.float32

    def nrm(k, shape, fan_in):
        return jax.random.normal(k, shape, f32) * (fan_in ** -0.5)

    def gain(k, shape):
        return 1.0 + 0.05 * jax.random.normal(k, shape, f32)

    return {
        "x": jax.random.normal(ks[0], (BATCH, SEQ, D_MODEL), f32),
        "ev_w_in": nrm(ks[1], (N_EVEN, D_MODEL, EVEN_IN), D_MODEL),
        "ev_g_cq": gain(ks[2], (N_EVEN, Q_LORA)),
        "ev_w_uq": nrm(ks[3], (N_EVEN, Q_LORA, MLA_HEADS * (MLA_NOPE + MLA_ROPE)), Q_LORA),
        "ev_g_ckv": gain(ks[4], (N_EVEN, KV_LORA)),
        "ev_w_ukv": nrm(ks[5], (N_EVEN, KV_LORA, MLA_HEADS * (MLA_NOPE + MLA_V)), KV_LORA),
        "ev_w_out": nrm(ks[6], (N_EVEN, MIX_EVEN, D_MODEL), MIX_EVEN),
        "od_w_qkv": nrm(ks[7], (N_ODD, D_MODEL, 3 * MIX_ODD), D_MODEL),
        "od_rel_bias": 0.1 * jax.random.normal(ks[8], (N_ODD, C_HEADS, 2 * REL_CLIP + 1), f32),
        "od_w_out": nrm(ks[9], (N_ODD, MIX_ODD, D_MODEL), MIX_ODD),
        "g_mix": gain(ks[10], (DEPTH, D_MODEL)),
        "g_ffn": gain(ks[11], (DEPTH, D_MODEL)),
        "w_gate": nrm(ks[12], (DEPTH, D_MODEL, D_FF), D_MODEL),
        "w_up": nrm(ks[13], (DEPTH, D_MODEL, D_FF), D_MODEL),
        "w_down": nrm(ks[14], (DEPTH, D_FF, D_MODEL), D_FF),
        "g_final": gain(ks[15], (D_MODEL,)),
    }


def reference(x, ev_w_in, ev_g_cq, ev_w_uq, ev_g_ckv, ev_w_ukv, ev_w_out,
              od_w_qkv, od_rel_bias, od_w_out, g_mix, g_ffn, w_gate, w_up, w_down, g_final):
    h = x
    for layer in range(DEPTH):
        u = rms_norm(h, g_mix[layer])
        if layer % 2 == 0:
            i = layer // 2
            h = h + mla_stick_breaking_mixer(u, ev_w_in[i], ev_g_cq[i], ev_w_uq[i],
                                             ev_g_ckv[i], ev_w_ukv[i], ev_w_out[i])
        else:
            i = layer // 2
            h = h + chunk_band_mixer(u, od_w_qkv[i], od_rel_bias[i], od_w_out[i])
        u = rms_norm(h, g_ffn[layer])
        h = h + swiglu(u, w_gate[layer], w_up[layer], w_down[layer])
    return rms_norm(h, g_final)
```

```python
import functools

import jax
import jax.numpy as jnp
from jax import lax
from jax.experimental import pallas as pl
from jax.experimental.pallas import tpu as pltpu

F32 = jnp.float32
BF16 = jnp.bfloat16

CHUNK = 64
MLA_HEADS = 8
MLA_NOPE = 64
MLA_ROPE = 32
MLA_V = 64
Q_LORA = 384
KV_LORA = 256
ROPE_THETA = 10000.0
SB_HEADS = 8
SB_DIM = 64
C_HEADS = 16
C_DIM = 64
LEFT_CHUNKS = 8
REL_CLIP = 256
RMS_EPS = 1e-6

LANES = 128
V7X_VMEM_BYTES = 64 * 1024 * 1024

ROW_TILE = 256
FFN_ROW_TILE = 512
FFN_COL_TILE = 256
ATT_TILE = 256
BAND_GROUP = ATT_TILE // CHUNK
BAND_KEYS = (LEFT_CHUNKS + BAND_GROUP) * CHUNK

NEG_BIG = -1e30


def _vmem_limit(nbytes):
    return int(min(max(2 * nbytes, 32 * 1024 * 1024), V7X_VMEM_BYTES - 8 * 1024 * 1024))


def _params(n_axes, nbytes):
    return pltpu.CompilerParams(
        dimension_semantics=("arbitrary",) * n_axes,
        vmem_limit_bytes=_vmem_limit(nbytes),
    )


def _dot(a, b):
    return jnp.dot(a, b, preferred_element_type=F32)


def _dot_nt(a, b):
    return lax.dot_general(a, b, (((1,), (1,)), ((), ())), preferred_element_type=F32)


def _rms(x, g):
    return x * lax.rsqrt(jnp.mean(x * x, axis=-1, keepdims=True) + RMS_EPS) * g


def _resident(shape):
    nd = len(shape)
    return pl.BlockSpec(shape, lambda *_: (0,) * nd, pipeline_mode=pl.Buffered(1))


def _rope(x, tab_ref):
    return (x * tab_ref[0]
            + pltpu.roll(x, MLA_ROPE // 2, 1) * tab_ref[1]
            + pltpu.roll(x, LANES - MLA_ROPE // 2, 1) * tab_ref[2])


def _even_proj_body(h_ref, gmix_ref, wcq_ref, wckv_ref, wkr_ref, wb_ref, gcq_ref, wuq_ref,
                    gckv_ref, wuk_ref, wuv_ref, ropeq_ref, ropek_ref,
                    qa_ref, ka_ref, va_ref, qb_ref, kb_ref, vb_ref):
    u = _rms(h_ref[...], gmix_ref[...]).astype(BF16)
    nb = SB_HEADS * SB_DIM
    b = _dot(u, wb_ref[...])
    qb_ref[...] = (b[:, :nb] * (SB_DIM ** -0.5)).astype(BF16)
    kb_ref[...] = b[:, nb:2 * nb].astype(BF16)
    vb_ref[...] = b[:, 2 * nb:].astype(BF16)

    cqn = _rms(_dot(u, wcq_ref[...]), gcq_ref[...]).astype(BF16)
    ckvn = _rms(_dot(u, wckv_ref[...]), gckv_ref[...]).astype(BF16)
    va_ref[...] = _dot(ckvn, wuv_ref[...]).astype(BF16)

    k_rope = _rope(_dot(u, wkr_ref[...]), ropek_ref)
    qa = _dot(cqn, wuq_ref[...])
    kn = _dot(ckvn, wuk_ref[...])
    for hh in range(MLA_HEADS):
        sl = slice(hh * LANES, (hh + 1) * LANES)
        qa_ref[:, sl] = _rope(qa[:, sl], ropeq_ref).astype(BF16)
        ka_ref[:, sl] = (kn[:, sl] + k_rope).astype(BF16)


def _even_proj(h, gmix, wcq, wckv, wkr, wb, gcq, wuq, gckv, wuk, wuv, ropeq, ropek, seq):
    t, d = h.shape
    tm = ROW_TILE
    per_seq = seq // tm
    row = lambda n: pl.BlockSpec((tm, n), lambda i: (i, 0))
    rope_spec = pl.BlockSpec((3, tm, LANES), lambda i: (0, i % per_seq, 0))
    weights = (wcq, wckv, wkr, wb, gcq, wuq, gckv, wuk, wuv)
    out_widths = (MLA_HEADS * LANES, MLA_HEADS * LANES, MLA_HEADS * MLA_V) + (SB_HEADS * SB_DIM,) * 3
    nbytes = (sum(w.size * w.dtype.itemsize for w in weights)
              + 2 * tm * d * 4 + 2 * tm * sum(out_widths) * 2 + tm * 8192 * 4)
    return pl.pallas_call(
        _even_proj_body,
        grid=(t // tm,),
        in_specs=[row(d), _resident(gmix.shape)]
                 + [_resident(wcq.shape), _resident(wckv.shape), _resident(wkr.shape), _resident(wb.shape),
                    _resident(gcq.shape), _resident(wuq.shape), _resident(gckv.shape), _resident(wuk.shape),
                    _resident(wuv.shape), rope_spec, rope_spec],
        out_specs=[row(n) for n in out_widths],
        out_shape=[jax.ShapeDtypeStruct((t, n), BF16) for n in out_widths],
        compiler_params=_params(1, nbytes),
        name="even_proj",
    )(h, gmix, wcq, wckv, wkr, wb, gcq, wuq, gckv, wuk, wuv, ropeq, ropek)


def _head_select(shape, head_in_pair):
    lane = lax.broadcasted_iota(jnp.int32, shape, 1)
    return (lane < SB_DIM) if head_in_pair == 0 else (lane >= SB_DIM)


def _mla_body(q_ref, k_ref, v_ref, o_ref):
    tq = ATT_TILE
    qi = pl.program_id(2)
    row_chunk = lax.broadcasted_iota(jnp.int32, (tq, tq), 0) // CHUNK
    col_chunk = lax.broadcasted_iota(jnp.int32, (tq, tq), 1) // CHUNK
    outs = []
    for hh in range(2):
        hsl = slice(hh * LANES, (hh + 1) * LANES)
        q = q_ref[0, :, hsl]

        def step(kb, carry, diagonal, hsl=hsl, q=q):
            m, l, acc = carry
            rows = pl.ds(pl.multiple_of(kb * tq, tq), tq)
            s = _dot_nt(q, k_ref[0, rows, hsl])
            if diagonal:
                s = jnp.where(col_chunk <= row_chunk, s, -jnp.inf)
            m_new = jnp.maximum(m, jnp.max(s, axis=-1, keepdims=True))
            alpha = jnp.exp(m - m_new)
            p = jnp.exp(s - m_new)
            l = alpha * l + jnp.sum(p, axis=-1, keepdims=True)
            acc = alpha * acc + _dot(p.astype(BF16), v_ref[0, rows, :])
            return m_new, l, acc

        init = (jnp.full((tq, 1), NEG_BIG, F32), jnp.zeros((tq, 1), F32), jnp.zeros((tq, LANES), F32))
        carry = lax.fori_loop(0, qi, functools.partial(step, diagonal=False), init)
        _, l, acc = step(qi, carry, True)
        outs.append(acc / l)
    o_ref[0] = jnp.where(_head_select((tq, LANES), 0), outs[0], outs[1]).astype(BF16)


def _mla_attention(qa, ka, va):
    bsz, seq, _ = qa.shape
    tq = ATT_TILE
    pairs = MLA_HEADS // 2
    nbytes = 2 * (seq * 2 * LANES * 2 + seq * LANES * 2 + tq * 3 * LANES * 2) + 16 * tq * tq * 4
    return pl.pallas_call(
        _mla_body,
        grid=(bsz, pairs, seq // tq),
        in_specs=[pl.BlockSpec((1, tq, 2 * LANES), lambda b, hp, qi: (b, qi, hp)),
                  pl.BlockSpec((1, seq, 2 * LANES), lambda b, hp, qi: (b, 0, hp)),
                  pl.BlockSpec((1, seq, LANES), lambda b, hp, qi: (b, 0, hp))],
        out_specs=pl.BlockSpec((1, tq, LANES), lambda b, hp, qi: (b, qi, hp)),
        out_shape=jax.ShapeDtypeStruct((bsz, seq, pairs * LANES), BF16),
        compiler_params=_params(3, nbytes),
        name="mla_attention",
    )(qa, ka, va)


def _sb_body(q_ref, k_ref, v_ref, tri_ref, o_ref):
    tq = ATT_TILE
    qi = pl.program_id(2)
    before = (lax.broadcasted_iota(jnp.int32, (tq, tq), 1)
              < lax.broadcasted_iota(jnp.int32, (tq, tq), 0))
    q_pair = q_ref[0]
    outs = []
    for hh in range(2):
        q = jnp.where(_head_select((tq, LANES), hh), q_pair, jnp.zeros_like(q_pair))

        def step(kb, carry, diagonal, q=q):
            later, acc = carry
            rows = pl.ds(pl.multiple_of(kb * tq, tq), tq)
            z = _dot_nt(q, k_ref[0, rows, :])
            softplus = jnp.log1p(jnp.exp(-jnp.abs(z)))
            log_beta = jnp.minimum(z, 0.0) - softplus
            log_keep = -jnp.maximum(z, 0.0) - softplus
            if diagonal:
                log_keep = jnp.where(before, log_keep, 0.0)
            hi = log_keep.astype(BF16)
            lo = (log_keep - hi.astype(F32)).astype(BF16)
            between = _dot(jnp.concatenate([hi, lo], axis=1), tri_ref[...]) + later
            w = jnp.exp(log_beta + between)
            if diagonal:
                w = jnp.where(before, w, 0.0)
            acc = acc + _dot(w.astype(BF16), v_ref[0, rows, :])
            later = later + jnp.sum(log_keep, axis=-1, keepdims=True)
            return later, acc

        carry = step(qi, (jnp.zeros((tq, 1), F32), jnp.zeros((tq, LANES), F32)), True)
        _, acc = lax.fori_loop(
            0, qi, lambda i, c: step(qi - 1 - i, c, False), carry)
        outs.append(acc)
    o_ref[0] = jnp.where(_head_select((tq, LANES), 0), outs[0], outs[1]).astype(BF16)


def _sb_attention(qb, kb, vb):
    bsz, seq, _ = qb.shape
    tq = ATT_TILE
    pairs = SB_HEADS // 2
    j = jnp.arange(tq)
    tri = (j[:, None] > j[None, :]).astype(BF16)
    tri2 = jnp.concatenate([tri, tri], axis=0)
    nbytes = 2 * (2 * seq * LANES * 2 + 2 * tq * LANES * 2) + 2 * tq * tq * 2 + 20 * tq * tq * 4
    return pl.pallas_call(
        _sb_body,
        grid=(bsz, pairs, seq // tq),
        in_specs=[pl.BlockSpec((1, tq, LANES), lambda b, hp, qi: (b, qi, hp)),
                  pl.BlockSpec((1, seq, LANES), lambda b, hp, qi: (b, 0, hp)),
                  pl.BlockSpec((1, seq, LANES), lambda b, hp, qi: (b, 0, hp)),
                  _resident(tri2.shape)],
        out_specs=pl.BlockSpec((1, tq, LANES), lambda b, hp, qi: (b, qi, hp)),
        out_shape=jax.ShapeDtypeStruct((bsz, seq, pairs * LANES), BF16),
        compiler_params=_params(3, nbytes),
        name="sb_attention",
    )(qb, kb, vb, tri2)


def _odd_proj_body(h_ref, gmix_ref, w_ref, q_ref, k_ref, v_ref):
    u = _rms(h_ref[...], gmix_ref[...]).astype(BF16)
    n = C_HEADS * C_DIM
    qkv = _dot(u, w_ref[...])
    q_ref[...] = (qkv[:, :n] * (C_DIM ** -0.5)).astype(BF16)
    k_ref[...] = qkv[:, n:2 * n].astype(BF16)
    v_ref[...] = qkv[:, 2 * n:].astype(BF16)


def _odd_proj(h, gmix, w_qkv):
    t, d = h.shape
    tm = ROW_TILE
    n = C_HEADS * C_DIM
    row = lambda m: pl.BlockSpec((tm, m), lambda i: (i, 0))
    nbytes = w_qkv.size * 2 + 2 * tm * d * 4 + 2 * 3 * tm * n * 2 + 2 * tm * 3 * n * 4
    return pl.pallas_call(
        _odd_proj_body,
        grid=(t // tm,),
        in_specs=[row(d), _resident(gmix.shape), _resident(w_qkv.shape)],
        out_specs=[row(n)] * 3,
        out_shape=[jax.ShapeDtypeStruct((t, n), BF16)] * 3,
        compiler_params=_params(1, nbytes),
        name="odd_proj",
    )(h, gmix, w_qkv)


def _band_body(q_ref, k_ref, v_ref, bias_ref, o_ref):
    tq = ATT_TILE

    def group(q0, k0, n_keys, bias_off):
        q_pair = q_ref[0, pl.ds(q0, tq), :]
        keys = k_ref[0, pl.ds(k0, n_keys), :]
        vals = v_ref[0, pl.ds(k0, n_keys), :]
        outs = []
        for hh in range(2):
            q = jnp.where(_head_select((tq, LANES), hh), q_pair, jnp.zeros_like(q_pair))
            s = _dot_nt(q, keys) + bias_ref[hh, :, bias_off:bias_off + n_keys]
            p = jnp.exp(s - jnp.max(s, axis=-1, keepdims=True))
            denom = jnp.sum(p, axis=-1, keepdims=True)
            outs.append(_dot(p.astype(BF16), vals) / denom)
        o_ref[0, pl.ds(q0, tq), :] = jnp.where(_head_select((tq, LANES), 0), outs[0], outs[1]).astype(BF16)

    lead = LEFT_CHUNKS // BAND_GROUP
    for g in range(lead):
        group(g * tq, 0, (g + 1) * tq, LEFT_CHUNKS * CHUNK - g * tq)

    def body(g, _):
        q0 = pl.multiple_of(g * tq, tq)
        group(q0, pl.multiple_of(q0 - LEFT_CHUNKS * CHUNK, tq), BAND_KEYS, 0)
        return 0

    lax.fori_loop(lead, q_ref.shape[1] // tq, body, 0)


def _band_bias_table(rel_bias):
    r = jnp.arange(ATT_TILE)[:, None]
    j = jnp.arange(BAND_KEYS)[None, :]
    rel = LEFT_CHUNKS * CHUNK + r - j
    idx = jnp.clip(rel, -REL_CLIP, REL_CLIP) + REL_CLIP
    q_chunk, k_chunk = r // CHUNK, j // CHUNK
    in_band = (k_chunk >= q_chunk) & (k_chunk <= q_chunk + LEFT_CHUNKS)
    return jnp.where(in_band[None], rel_bias[:, idx], -jnp.inf).astype(F32)


def _band_attention(q, k, v, bias_table):
    bsz, seq, _ = q.shape
    pairs = C_HEADS // 2
    tq = ATT_TILE
    seq_spec = pl.BlockSpec((1, seq, LANES), lambda b, hp: (b, 0, hp))
    nbytes = 2 * (4 * seq * LANES * 2 + 2 * tq * BAND_KEYS * 4) + 8 * tq * BAND_KEYS * 4
    return pl.pallas_call(
        _band_body,
        grid=(bsz, pairs),
        in_specs=[seq_spec, seq_spec, seq_spec,
                  pl.BlockSpec((2, tq, BAND_KEYS), lambda b, hp: (hp, 0, 0))],
        out_specs=seq_spec,
        out_shape=jax.ShapeDtypeStruct((bsz, seq, pairs * LANES), BF16),
        compiler_params=_params(2, nbytes),
        name="band_attention",
    )(q, k, v, bias_table)


def _out_proj_body(*refs, n_in):
    h_ref, g_ref = refs[0], refs[1]
    o_refs = refs[2:2 + n_in]
    w_refs = refs[2 + n_in:2 + 2 * n_in]
    hout_ref, u_ref = refs[2 + 2 * n_in:]
    h = h_ref[...]
    for o_ref, w_ref in zip(o_refs, w_refs):
        h = h + _dot(o_ref[...], w_ref[...])
    hout_ref[...] = h
    u_ref[...] = _rms(h, g_ref[...]).astype(BF16)


def _out_proj(h, g_ffn, outs, weights):
    t, d = h.shape
    tm = ROW_TILE
    row = lambda n: pl.BlockSpec((tm, n), lambda i: (i, 0))
    nbytes = (sum(w.size * 2 for w in weights) + 2 * tm * d * (4 + 4 + 2)
              + 2 * tm * sum(o.shape[1] for o in outs) * 2 + 2 * tm * d * 4)
    return pl.pallas_call(
        functools.partial(_out_proj_body, n_in=len(outs)),
        grid=(t // tm,),
        in_specs=[row(d), _resident(g_ffn.shape)] + [row(o.shape[1]) for o in outs]
                 + [_resident(w.shape) for w in weights],
        out_specs=[row(d), row(d)],
        out_shape=[jax.ShapeDtypeStruct((t, d), F32), jax.ShapeDtypeStruct((t, d), BF16)],
        compiler_params=_params(1, nbytes),
        name="out_proj",
    )(h, g_ffn, *outs, *weights)


def _ffn_body(u_ref, h_ref, wg_ref, wu_ref, wd_ref, gfin_ref, o_ref, acc_ref, *, final_norm):
    u = u_ref[...]
    d_ff = wg_ref.shape[1]
    acc_ref[...] = h_ref[...]
    for f0 in range(0, d_ff, FFN_COL_TILE):
        cols = slice(f0, f0 + FFN_COL_TILE)
        gate = _dot(u, wg_ref[:, cols])
        act = (gate * jax.nn.sigmoid(gate) * _dot(u, wu_ref[:, cols])).astype(BF16)
        acc_ref[...] += _dot(act, wd_ref[cols, :])
    out = acc_ref[...]
    o_ref[...] = _rms(out, gfin_ref[...]) if final_norm else out


def _ffn(u, h, wg, wu, wd, g_final, final_norm):
    t, d = h.shape
    tm = FFN_ROW_TILE
    d_ff = wg.shape[1]
    assert d_ff % FFN_COL_TILE == 0
    row = pl.BlockSpec((tm, d), lambda i: (i, 0))
    nbytes = 3 * d * d_ff * 2 + 2 * tm * d * (2 + 4 + 4) + tm * d * 4 + 4 * tm * FFN_COL_TILE * 4
    return pl.pallas_call(
        functools.partial(_ffn_body, final_norm=final_norm),
        grid=(t // tm,),
        in_specs=[row, row, _resident(wg.shape), _resident(wu.shape), _resident(wd.shape),
                  _resident(g_final.shape)],
        out_specs=row,
        out_shape=jax.ShapeDtypeStruct((t, d), F32),
        scratch_shapes=[pltpu.VMEM((tm, d), F32)],
        compiler_params=_params(1, nbytes),
        name="swiglu",
    )(u, h, wg, wu, wd, g_final)


def _head_slots(w, n_heads, per_head, keep):
    k = w.shape[0]
    w = w.reshape(k, n_heads, per_head)[:, :, :keep]
    return jnp.pad(w, ((0, 0), (0, 0), (0, LANES - keep))).reshape(k, n_heads * LANES)


def _rope_slot_tables(seq, scale):
    half = MLA_ROPE // 2
    pos = jnp.arange(seq, dtype=F32)
    inv_freq = ROPE_THETA ** (-jnp.arange(0, MLA_ROPE, 2, dtype=F32) / MLA_ROPE)
    ang = pos[:, None] * inv_freq[None, :]
    cos, sin = jnp.cos(ang), jnp.sin(ang)
    zeros = jnp.zeros((seq, half), F32)
    pad = jnp.zeros((seq, LANES - MLA_NOPE - MLA_ROPE), F32)
    ones = jnp.ones((seq, MLA_NOPE), F32)
    direct = jnp.concatenate([ones, cos, cos, pad], axis=1)
    from_lower = jnp.concatenate([0 * ones, zeros, sin, pad], axis=1)
    from_upper = jnp.concatenate([0 * ones, -sin, zeros, pad], axis=1)
    return jnp.stack([direct, from_lower, from_upper]) * scale


def kernel(x, ev_w_in, ev_g_cq, ev_w_uq, ev_g_ckv, ev_w_ukv, ev_w_out, od_w_qkv, od_rel_bias, od_w_out,
           g_mix, g_ffn, w_gate, w_up, w_down, g_final):
    bsz, seq, d = x.shape
    depth = g_mix.shape[0]
    t = bsz * seq
    h = x.reshape(t, d)
    g_fin = g_final.reshape(1, d)
    rope_q = _rope_slot_tables(seq, (MLA_NOPE + MLA_ROPE) ** -0.5)
    rope_k = _rope_slot_tables(seq, 1.0)

    for layer in range(depth):
        i = layer // 2
        gm = g_mix[layer].reshape(1, d)
        if layer % 2 == 0:
            w_in = ev_w_in[i]
            o1, o2, o3 = Q_LORA, Q_LORA + KV_LORA, Q_LORA + KV_LORA + MLA_ROPE
            wcq = w_in[:, :o1].astype(BF16)
            wckv = w_in[:, o1:o2].astype(BF16)
            wkr = jnp.pad(w_in[:, o2:o3], ((0, 0), (MLA_NOPE, LANES - MLA_NOPE - MLA_ROPE))).astype(BF16)
            wb = w_in[:, o3:].astype(BF16)
            wuq = _head_slots(ev_w_uq[i], MLA_HEADS, MLA_NOPE + MLA_ROPE, MLA_NOPE + MLA_ROPE).astype(BF16)
            wuk = _head_slots(ev_w_ukv[i], MLA_HEADS, MLA_NOPE + MLA_V, MLA_NOPE).astype(BF16)
            wuv = ev_w_ukv[i].reshape(KV_LORA, MLA_HEADS, MLA_NOPE + MLA_V)[:, :, MLA_NOPE:]
            wuv = wuv.reshape(KV_LORA, MLA_HEADS * MLA_V).astype(BF16)
            qa, ka, va, qb, kb, vb = _even_proj(
                h, gm, wcq, wckv, wkr, wb, ev_g_cq[i].reshape(1, -1), wuq, ev_g_ckv[i].reshape(1, -1),
                wuk, wuv, rope_q, rope_k, seq)
            r3 = lambda a: a.reshape(bsz, seq, a.shape[1])
            o_a = _mla_attention(r3(qa), r3(ka), r3(va)).reshape(t, -1)
            o_b = _sb_attention(r3(qb), r3(kb), r3(vb)).reshape(t, -1)
            w_out = ev_w_out[i].astype(BF16)
            na = MLA_HEADS * MLA_V
            mix_outs, mix_w = [o_a, o_b], [w_out[:na], w_out[na:]]
        else:
            q, k, v = _odd_proj(h, gm, od_w_qkv[i].astype(BF16))
            r3 = lambda a: a.reshape(bsz, seq, a.shape[1])
            o_c = _band_attention(r3(q), r3(k), r3(v), _band_bias_table(od_rel_bias[i])).reshape(t, -1)
            mix_outs, mix_w = [o_c], [od_w_out[i].astype(BF16)]
        h, u = _out_proj(h, g_ffn[layer].reshape(1, d), mix_outs, mix_w)
        h = _ffn(u, h, w_gate[layer].astype(BF16), w_up[layer].astype(BF16), w_down[layer].astype(BF16),
                 g_fin, final_norm=(layer == depth - 1))
    return h.reshape(bsz, seq, d)
```

```python
import functools

import jax
import jax.numpy as jnp
from jax import lax
from jax.experimental import pallas as pl
from jax.experimental.pallas import tpu as pltpu

F32 = jnp.float32
BF16 = jnp.bfloat16

CHUNK = 64
MLA_HEADS = 8
MLA_NOPE = 64
MLA_ROPE = 32
MLA_V = 64
Q_LORA = 384
KV_LORA = 256
ROPE_THETA = 10000.0
SB_HEADS = 8
SB_DIM = 64
C_HEADS = 16
C_DIM = 64
LEFT_CHUNKS = 8
REL_CLIP = 256
RMS_EPS = 1e-6

LANES = 128
V7X_VMEM_BYTES = 64 * 1024 * 1024

ROW_TILE = 256
FFN_ROW_TILE = 512
FFN_COL_TILE = 256
Q_TILE = 512
K_TILE = 256
ATT_TILE = 256
BAND_GROUP = ATT_TILE // CHUNK
BAND_KEYS = (LEFT_CHUNKS + BAND_GROUP) * CHUNK

NEG_BIG = -1e30


def _vmem_limit(nbytes):
    return int(min(max(2 * nbytes, 32 * 1024 * 1024), V7X_VMEM_BYTES - 8 * 1024 * 1024))


def _params(n_axes, nbytes):
    return pltpu.CompilerParams(
        dimension_semantics=("arbitrary",) * n_axes,
        vmem_limit_bytes=_vmem_limit(nbytes),
    )


def _dot(a, b):
    return jnp.dot(a, b, preferred_element_type=F32)


def _dot_nt(a, b):
    return lax.dot_general(a, b, (((1,), (1,)), ((), ())), preferred_element_type=F32)


def _rms(x, g):
    return x * lax.rsqrt(jnp.mean(x * x, axis=-1, keepdims=True) + RMS_EPS) * g


def _resident(shape):
    nd = len(shape)
    return pl.BlockSpec(shape, lambda *_: (0,) * nd, pipeline_mode=pl.Buffered(1))


def _rope(x, tab_ref):
    return (x * tab_ref[0]
            + pltpu.roll(x, MLA_ROPE // 2, 1) * tab_ref[1]
            + pltpu.roll(x, LANES - MLA_ROPE // 2, 1) * tab_ref[2])


def _even_proj_body(h_ref, gmix_ref, wcq_ref, wckv_ref, wkr_ref, wb_ref, gcq_ref, wuq_ref,
                    gckv_ref, wuk_ref, wuv_ref, ropeq_ref, ropek_ref,
                    qa_ref, ka_ref, va_ref, qb_ref, kb_ref, vb_ref):
    u = _rms(h_ref[...], gmix_ref[...]).astype(BF16)
    nb = SB_HEADS * SB_DIM
    b = _dot(u, wb_ref[...])
    qb_ref[...] = (b[:, :nb] * (SB_DIM ** -0.5)).astype(BF16)
    kb_ref[...] = b[:, nb:2 * nb].astype(BF16)
    vb_ref[...] = b[:, 2 * nb:].astype(BF16)

    cqn = _rms(_dot(u, wcq_ref[...]), gcq_ref[...]).astype(BF16)
    ckvn = _rms(_dot(u, wckv_ref[...]), gckv_ref[...]).astype(BF16)
    va = _dot(ckvn, wuv_ref[...])
    ones_lane = lax.broadcasted_iota(jnp.int32, va.shape, 1) % LANES == MLA_V
    va_ref[...] = jnp.where(ones_lane, 1.0, va).astype(BF16)

    k_rope = _rope(_dot(u, wkr_ref[...]), ropek_ref)
    qa = _dot(cqn, wuq_ref[...])
    kn = _dot(ckvn, wuk_ref[...])
    for hh in range(MLA_HEADS):
        sl = slice(hh * LANES, (hh + 1) * LANES)
        qa_ref[:, sl] = _rope(qa[:, sl], ropeq_ref).astype(BF16)
        ka_ref[:, sl] = (kn[:, sl] + k_rope).astype(BF16)


def _even_proj(h, gmix, wcq, wckv, wkr, wb, gcq, wuq, gckv, wuk, wuv, ropeq, ropek, seq):
    t, d = h.shape
    tm = ROW_TILE
    per_seq = seq // tm
    row = lambda n: pl.BlockSpec((tm, n), lambda i: (i, 0))
    rope_spec = pl.BlockSpec((3, tm, LANES), lambda i: (0, i % per_seq, 0))
    weights = (wcq, wckv, wkr, wb, gcq, wuq, gckv, wuk, wuv)
    out_widths = (MLA_HEADS * LANES,) * 3 + (SB_HEADS * SB_DIM,) * 3
    nbytes = (sum(w.size * w.dtype.itemsize for w in weights)
              + 2 * tm * d * 4 + 2 * tm * sum(out_widths) * 2 + tm * 8192 * 4)
    return pl.pallas_call(
        _even_proj_body,
        grid=(t // tm,),
        in_specs=[row(d), _resident(gmix.shape)]
                 + [_resident(wcq.shape), _resident(wckv.shape), _resident(wkr.shape), _resident(wb.shape),
                    _resident(gcq.shape), _resident(wuq.shape), _resident(gckv.shape), _resident(wuk.shape),
                    _resident(wuv.shape), rope_spec, rope_spec],
        out_specs=[row(n) for n in out_widths],
        out_shape=[jax.ShapeDtypeStruct((t, n), BF16) for n in out_widths],
        compiler_params=_params(1, nbytes),
        name="even_proj",
    )(h, gmix, wcq, wckv, wkr, wb, gcq, wuq, gckv, wuk, wuv, ropeq, ropek)


def _head_select(shape, head_in_pair):
    lane = lax.broadcasted_iota(jnp.int32, shape, 1)
    return (lane < SB_DIM) if head_in_pair == 0 else (lane >= SB_DIM)


def _mla_body(q_ref, k_ref, v_ref, o_ref):
    tq, tk = Q_TILE, K_TILE
    qi = pl.program_id(2)
    per = tq // tk
    row_chunk = lax.broadcasted_iota(jnp.int32, (tq, tk), 0) // CHUNK
    col_chunk = lax.broadcasted_iota(jnp.int32, (tq, tk), 1) // CHUNK
    slots = [slice(hh * LANES, (hh + 1) * LANES) for hh in range(2)]
    qs = [q_ref[0, :, sl] for sl in slots]

    def tile(kb, carry, visible):
        rows = pl.ds(pl.multiple_of(kb * tk, tk), tk)
        new = []
        for hh in range(2):
            m, acc = carry[hh]
            s = _dot_nt(qs[hh], k_ref[0, rows, slots[hh]])
            if visible is not None:
                s = jnp.where(visible, s, -jnp.inf)
            m_new = jnp.maximum(m, jnp.max(s, axis=-1, keepdims=True))
            p = jnp.exp(s - m_new)
            acc = jnp.exp(m - m_new) * acc + _dot(p.astype(BF16), v_ref[0, rows, slots[hh]])
            new.append((m_new, acc))
        return tuple(new)

    init = ((jnp.full((tq, 1), NEG_BIG, F32), jnp.zeros((tq, LANES), F32)),) * 2
    carry = lax.fori_loop(0, per * qi, lambda kb, c: tile(kb, c, None), init)
    for d in range(per):
        carry = tile(per * qi + d, carry, col_chunk + d * (tk // CHUNK) <= row_chunk)
    outs = [acc / acc[:, MLA_V:MLA_V + 1] for _, acc in carry]
    lane = lax.broadcasted_iota(jnp.int32, (tq, LANES), 1)
    o_ref[0] = jnp.where(lane < MLA_V, outs[0], pltpu.roll(outs[1], MLA_V, 1)).astype(BF16)


def _mla_attention(qa, ka, va):
    bsz, seq, _ = qa.shape
    tq = Q_TILE
    pairs = MLA_HEADS // 2
    nbytes = 2 * (2 * seq * 2 * LANES * 2 + tq * 3 * LANES * 2) + 16 * tq * K_TILE * 4
    return pl.pallas_call(
        _mla_body,
        grid=(bsz, pairs, seq // tq),
        in_specs=[pl.BlockSpec((1, tq, 2 * LANES), lambda b, hp, qi: (b, qi, hp)),
                  pl.BlockSpec((1, seq, 2 * LANES), lambda b, hp, qi: (b, 0, hp)),
                  pl.BlockSpec((1, seq, 2 * LANES), lambda b, hp, qi: (b, 0, hp))],
        out_specs=pl.BlockSpec((1, tq, LANES), lambda b, hp, qi: (b, qi, hp)),
        out_shape=jax.ShapeDtypeStruct((bsz, seq, pairs * LANES), BF16),
        compiler_params=_params(3, nbytes),
        name="mla_attention",
    )(qa, ka, va)


def _sb_body(q_ref, k_ref, v_ref, tri_ref, o_ref):
    tq, tk = Q_TILE, K_TILE
    qi = pl.program_id(2)
    per = tq // tk
    row = lax.broadcasted_iota(jnp.int32, (tq, tk), 0)
    col = lax.broadcasted_iota(jnp.int32, (tq, tk), 1)
    q_pair = q_ref[0]
    qs = [jnp.where(_head_select((tq, LANES), hh), q_pair, jnp.zeros_like(q_pair)) for hh in range(2)]

    def tile(kb, carry, before):
        rows = pl.ds(pl.multiple_of(kb * tk, tk), tk)
        keys, vals = k_ref[0, rows, :], v_ref[0, rows, :]
        new = []
        for hh in range(2):
            later, acc = carry[hh]
            z = _dot_nt(qs[hh], keys)
            log_beta = jnp.minimum(z, 0.0) - jnp.log(1.0 + jnp.exp(-jnp.abs(z)))
            log_keep = log_beta - z
            if before is not None:
                log_keep = jnp.where(before, log_keep, 0.0)
            hi = log_keep.astype(BF16)
            lo = (log_keep - hi.astype(F32)).astype(BF16)
            between = _dot(jnp.concatenate([hi, lo], axis=1), tri_ref[...]) + later
            w = jnp.exp(log_beta + between)
            if before is not None:
                w = jnp.where(before, w, 0.0)
            acc = acc + _dot(w.astype(BF16), vals)
            later = between[:, 0:1] + log_keep[:, 0:1]
            new.append((later, acc))
        return tuple(new)

    carry = ((jnp.zeros((tq, 1), F32), jnp.zeros((tq, LANES), F32)),) * 2
    for d in reversed(range(per)):
        carry = tile(per * qi + d, carry, col + d * tk < row)
    carry = lax.fori_loop(0, per * qi, lambda i, c: tile(per * qi - 1 - i, c, None), carry)
    o_ref[0] = jnp.where(_head_select((tq, LANES), 0), carry[0][1], carry[1][1]).astype(BF16)


def _sb_attention(qb, kb, vb):
    bsz, seq, _ = qb.shape
    tq, tk = Q_TILE, K_TILE
    pairs = SB_HEADS // 2
    j = jnp.arange(tk)
    tri = (j[:, None] > j[None, :]).astype(BF16)
    tri2 = jnp.concatenate([tri, tri], axis=0)
    nbytes = 2 * (2 * seq * LANES * 2 + 2 * tq * LANES * 2) + 2 * tk * tk * 2 + 20 * tq * tk * 4
    return pl.pallas_call(
        _sb_body,
        grid=(bsz, pairs, seq // tq),
        in_specs=[pl.BlockSpec((1, tq, LANES), lambda b, hp, qi: (b, qi, hp)),
                  pl.BlockSpec((1, seq, LANES), lambda b, hp, qi: (b, 0, hp)),
                  pl.BlockSpec((1, seq, LANES), lambda b, hp, qi: (b, 0, hp)),
                  _resident(tri2.shape)],
        out_specs=pl.BlockSpec((1, tq, LANES), lambda b, hp, qi: (b, qi, hp)),
        out_shape=jax.ShapeDtypeStruct((bsz, seq, pairs * LANES), BF16),
        compiler_params=_params(3, nbytes),
        name="sb_attention",
    )(qb, kb, vb, tri2)


def _odd_proj_body(h_ref, gmix_ref, w_ref, q_ref, k_ref, v_ref):
    u = _rms(h_ref[...], gmix_ref[...]).astype(BF16)
    n = C_HEADS * C_DIM
    qkv = _dot(u, w_ref[...])
    q_ref[...] = (qkv[:, :n] * (C_DIM ** -0.5)).astype(BF16)
    k_ref[...] = qkv[:, n:2 * n].astype(BF16)
    v_ref[...] = qkv[:, 2 * n:].astype(BF16)


def _odd_proj(h, gmix, w_qkv):
    t, d = h.shape
    tm = ROW_TILE
    n = C_HEADS * C_DIM
    row = lambda m: pl.BlockSpec((tm, m), lambda i: (i, 0))
    nbytes = w_qkv.size * 2 + 2 * tm * d * 4 + 2 * 3 * tm * n * 2 + 2 * tm * 3 * n * 4
    return pl.pallas_call(
        _odd_proj_body,
        grid=(t // tm,),
        in_specs=[row(d), _resident(gmix.shape), _resident(w_qkv.shape)],
        out_specs=[row(n)] * 3,
        out_shape=[jax.ShapeDtypeStruct((t, n), BF16)] * 3,
        compiler_params=_params(1, nbytes),
        name="odd_proj",
    )(h, gmix, w_qkv)


def _band_body(q_ref, k_ref, v_ref, bias_ref, o_ref):
    tq = ATT_TILE

    def group(q0, k0, n_keys, bias_off):
        q_pair = q_ref[0, pl.ds(q0, tq), :]
        keys = k_ref[0, pl.ds(k0, n_keys), :]
        vals = v_ref[0, pl.ds(k0, n_keys), :]
        outs = []
        for hh in range(2):
            q = jnp.where(_head_select((tq, LANES), hh), q_pair, jnp.zeros_like(q_pair))
            s = _dot_nt(q, keys) + bias_ref[hh, :, bias_off:bias_off + n_keys]
            p = jnp.exp(s - jnp.max(s, axis=-1, keepdims=True))
            denom = jnp.sum(p, axis=-1, keepdims=True)
            outs.append(_dot(p.astype(BF16), vals) / denom)
        o_ref[0, pl.ds(q0, tq), :] = jnp.where(_head_select((tq, LANES), 0), outs[0], outs[1]).astype(BF16)

    lead = LEFT_CHUNKS // BAND_GROUP
    for g in range(lead):
        group(g * tq, 0, (g + 1) * tq, LEFT_CHUNKS * CHUNK - g * tq)

    def body(g, _):
        q0 = pl.multiple_of(g * tq, tq)
        group(q0, pl.multiple_of(q0 - LEFT_CHUNKS * CHUNK, tq), BAND_KEYS, 0)
        return 0

    lax.fori_loop(lead, q_ref.shape[1] // tq, body, 0)


def _band_bias_table(rel_bias):
    tq, n_heads = ATT_TILE, rel_bias.shape[0]
    shift = LEFT_CHUNKS * CHUNK
    assert shift - (BAND_KEYS - 1) >= -REL_CLIP
    n_rel = tq - 1 + BAND_KEYS
    n_clipped = n_rel - (tq - 1 + REL_CLIP + 1)
    by_rel = jnp.concatenate(
        [rel_bias[:, REL_CLIP - (tq - 1):], jnp.broadcast_to(rel_bias[:, -1:], (n_heads, n_clipped))], axis=1)
    period = jnp.pad(by_rel[:, ::-1], ((0, 0), (0, 1)))
    flat = jnp.tile(period, (1, tq))[:, :tq * n_rel]
    skewed = flat.reshape(n_heads, tq, n_rel)
    start = (n_rel - 1) - (shift + tq - 1)
    bias = skewed[:, :, start:start + BAND_KEYS]
    q_chunk = jnp.arange(tq)[:, None] // CHUNK
    k_chunk = jnp.arange(BAND_KEYS)[None, :] // CHUNK
    in_band = (k_chunk >= q_chunk) & (k_chunk <= q_chunk + LEFT_CHUNKS)
    return jnp.where(in_band[None], bias, -jnp.inf).astype(F32)


def _band_attention(q, k, v, bias_table):
    bsz, seq, _ = q.shape
    pairs = C_HEADS // 2
    tq = ATT_TILE
    seq_spec = pl.BlockSpec((1, seq, LANES), lambda b, hp: (b, 0, hp))
    nbytes = 2 * (4 * seq * LANES * 2 + 2 * tq * BAND_KEYS * 4) + 8 * tq * BAND_KEYS * 4
    return pl.pallas_call(
        _band_body,
        grid=(bsz, pairs),
        in_specs=[seq_spec, seq_spec, seq_spec,
                  pl.BlockSpec((2, tq, BAND_KEYS), lambda b, hp: (hp, 0, 0))],
        out_specs=seq_spec,
        out_shape=jax.ShapeDtypeStruct((bsz, seq, pairs * LANES), BF16),
        compiler_params=_params(2, nbytes),
        name="band_attention",
    )(q, k, v, bias_table)


def _out_proj_body(*refs, n_in):
    h_ref, g_ref = refs[0], refs[1]
    o_refs = refs[2:2 + n_in]
    w_refs = refs[2 + n_in:2 + 2 * n_in]
    hout_ref, u_ref = refs[2 + 2 * n_in:]
    h = h_ref[...]
    for o_ref, w_ref in zip(o_refs, w_refs):
        h = h + _dot(o_ref[...], w_ref[...])
    hout_ref[...] = h
    u_ref[...] = _rms(h, g_ref[...]).astype(BF16)


def _out_proj(h, g_ffn, outs, weights):
    t, d = h.shape
    tm = ROW_TILE
    row = lambda n: pl.BlockSpec((tm, n), lambda i: (i, 0))
    nbytes = (sum(w.size * 2 for w in weights) + 2 * tm * d * (4 + 4 + 2)
              + 2 * tm * sum(o.shape[1] for o in outs) * 2 + 2 * tm * d * 4)
    return pl.pallas_call(
        functools.partial(_out_proj_body, n_in=len(outs)),
        grid=(t // tm,),
        in_specs=[row(d), _resident(g_ffn.shape)] + [row(o.shape[1]) for o in outs]
                 + [_resident(w.shape) for w in weights],
        out_specs=[row(d), row(d)],
        out_shape=[jax.ShapeDtypeStruct((t, d), F32), jax.ShapeDtypeStruct((t, d), BF16)],
        compiler_params=_params(1, nbytes),
        name="out_proj",
    )(h, g_ffn, *outs, *weights)


def _ffn_body(u_ref, h_ref, wg_ref, wu_ref, wd_ref, gfin_ref, o_ref, acc_ref, *, final_norm):
    u = u_ref[...]
    d_ff = wg_ref.shape[1]
    acc_ref[...] = h_ref[...]
    for f0 in range(0, d_ff, FFN_COL_TILE):
        cols = slice(f0, f0 + FFN_COL_TILE)
        gate = _dot(u, wg_ref[:, cols])
        act = (gate * jax.nn.sigmoid(gate) * _dot(u, wu_ref[:, cols])).astype(BF16)
        acc_ref[...] += _dot(act, wd_ref[cols, :])
    out = acc_ref[...]
    o_ref[...] = _rms(out, gfin_ref[...]) if final_norm else out


def _ffn(u, h, wg, wu, wd, g_final, final_norm):
    t, d = h.shape
    tm = FFN_ROW_TILE
    d_ff = wg.shape[1]
    assert d_ff % FFN_COL_TILE == 0
    row = pl.BlockSpec((tm, d), lambda i: (i, 0))
    nbytes = 3 * d * d_ff * 2 + 2 * tm * d * (2 + 4 + 4) + tm * d * 4 + 4 * tm * FFN_COL_TILE * 4
    return pl.pallas_call(
        functools.partial(_ffn_body, final_norm=final_norm),
        grid=(t // tm,),
        in_specs=[row, row, _resident(wg.shape), _resident(wu.shape), _resident(wd.shape),
                  _resident(g_final.shape)],
        out_specs=row,
        out_shape=jax.ShapeDtypeStruct((t, d), F32),
        scratch_shapes=[pltpu.VMEM((tm, d), F32)],
        compiler_params=_params(1, nbytes),
        name="swiglu",
    )(u, h, wg, wu, wd, g_final)


def _head_slots(w, n_heads, per_head, start, keep):
    k = w.shape[0]
    w = w.reshape(k, n_heads, per_head)[:, :, start:start + keep]
    return jnp.pad(w, ((0, 0), (0, 0), (0, LANES - keep))).reshape(k, n_heads * LANES)


def _rope_slot_tables(seq, scale):
    half = MLA_ROPE // 2
    pos = jnp.arange(seq, dtype=F32)
    inv_freq = ROPE_THETA ** (-jnp.arange(0, MLA_ROPE, 2, dtype=F32) / MLA_ROPE)
    ang = pos[:, None] * inv_freq[None, :]
    cos, sin = jnp.cos(ang), jnp.sin(ang)
    zeros = jnp.zeros((seq, half), F32)
    pad = jnp.zeros((seq, LANES - MLA_NOPE - MLA_ROPE), F32)
    ones = jnp.ones((seq, MLA_NOPE), F32)
    direct = jnp.concatenate([ones, cos, cos, pad], axis=1)
    from_lower = jnp.concatenate([0 * ones, zeros, sin, pad], axis=1)
    from_upper = jnp.concatenate([0 * ones, -sin, zeros, pad], axis=1)
    return jnp.stack([direct, from_lower, from_upper]) * scale


def kernel(x, ev_w_in, ev_g_cq, ev_w_uq, ev_g_ckv, ev_w_ukv, ev_w_out, od_w_qkv, od_rel_bias, od_w_out,
           g_mix, g_ffn, w_gate, w_up, w_down, g_final):
    bsz, seq, d = x.shape
    depth = g_mix.shape[0]
    t = bsz * seq
    h = x.reshape(t, d)
    g_fin = g_final.reshape(1, d)
    rope_q = _rope_slot_tables(seq, (MLA_NOPE + MLA_ROPE) ** -0.5)
    rope_k = _rope_slot_tables(seq, 1.0)

    for layer in range(depth):
        i = layer // 2
        gm = g_mix[layer].reshape(1, d)
        if layer % 2 == 0:
            w_in = ev_w_in[i]
            o1, o2, o3 = Q_LORA, Q_LORA + KV_LORA, Q_LORA + KV_LORA + MLA_ROPE
            wcq = w_in[:, :o1].astype(BF16)
            wckv = w_in[:, o1:o2].astype(BF16)
            wkr = jnp.pad(w_in[:, o2:o3], ((0, 0), (MLA_NOPE, LANES - MLA_NOPE - MLA_ROPE))).astype(BF16)
            wb = w_in[:, o3:].astype(BF16)
            wuq = _head_slots(ev_w_uq[i], MLA_HEADS, MLA_NOPE + MLA_ROPE, 0, MLA_NOPE + MLA_ROPE).astype(BF16)
            wuk = _head_slots(ev_w_ukv[i], MLA_HEADS, MLA_NOPE + MLA_V, 0, MLA_NOPE).astype(BF16)
            wuv = _head_slots(ev_w_ukv[i], MLA_HEADS, MLA_NOPE + MLA_V, MLA_NOPE, MLA_V).astype(BF16)
            qa, ka, va, qb, kb, vb = _even_proj(
                h, gm, wcq, wckv, wkr, wb, ev_g_cq[i].reshape(1, -1), wuq, ev_g_ckv[i].reshape(1, -1),
                wuk, wuv, rope_q, rope_k, seq)
            r3 = lambda a: a.reshape(bsz, seq, a.shape[1])
            o_a = _mla_attention(r3(qa), r3(ka), r3(va)).reshape(t, -1)
            o_b = _sb_attention(r3(qb), r3(kb), r3(vb)).reshape(t, -1)
            w_out = ev_w_out[i].astype(BF16)
            na = MLA_HEADS * MLA_V
            mix_outs, mix_w = [o_a, o_b], [w_out[:na], w_out[na:]]
        else:
            q, k, v = _odd_proj(h, gm, od_w_qkv[i].astype(BF16))
            r3 = lambda a: a.reshape(bsz, seq, a.shape[1])
            o_c = _band_attention(r3(q), r3(k), r3(v), _band_bias_table(od_rel_bias[i])).reshape(t, -1)
            mix_outs, mix_w = [o_c], [od_w_out[i].astype(BF16)]
        h, u = _out_proj(h, g_ffn[layer].reshape(1, d), mix_outs, mix_w)
        h = _ffn(u, h, w_gate[layer].astype(BF16), w_up[layer].astype(BF16), w_down[layer].astype(BF16),
                 g_fin, final_norm=(layer == depth - 1))
    return h.reshape(bsz, seq, d)
```

```python
import functools

import jax
import jax.numpy as jnp
from jax import lax
from jax.experimental import pallas as pl
from jax.experimental.pallas import tpu as pltpu

F32 = jnp.float32
BF16 = jnp.bfloat16

CHUNK = 64
MLA_HEADS = 8
MLA_NOPE = 64
MLA_ROPE = 32
MLA_V = 64
Q_LORA = 384
KV_LORA = 256
ROPE_THETA = 10000.0
SB_HEADS = 8
SB_DIM = 64
C_HEADS = 16
C_DIM = 64
LEFT_CHUNKS = 8
REL_CLIP = 256
RMS_EPS = 1e-6

LANES = 128
V7X_VMEM_BYTES = 64 * 1024 * 1024

ROW_TILE = 256
FFN_ROW_TILE = 512
FFN_COL_TILE = 256
Q_TILE = 512
K_TILE = 256
HEADS_PER_STEP = 4
MLA_K_TILE = 512
MLA_HEADS_PER_STEP = 4
BAND_HEADS_PER_STEP = 4
ATT_TILE = 256
BAND_GROUP = ATT_TILE // CHUNK
BAND_KEYS = (LEFT_CHUNKS + BAND_GROUP) * CHUNK

NEG_BIG = -1e30
LOG2E = 1.4426950408889634


def _vmem_limit(nbytes):
    return int(min(max(2 * nbytes, 32 * 1024 * 1024), V7X_VMEM_BYTES - 8 * 1024 * 1024))


def _params(n_axes, nbytes):
    return pltpu.CompilerParams(
        dimension_semantics=("arbitrary",) * n_axes,
        vmem_limit_bytes=_vmem_limit(nbytes),
    )


def _dot(a, b):
    return jnp.dot(a, b, preferred_element_type=F32)


def _dot_nt(a, b):
    return lax.dot_general(a, b, (((1,), (1,)), ((), ())), preferred_element_type=F32)


def _rms(x, g):
    return x * lax.rsqrt(jnp.mean(x * x, axis=-1, keepdims=True) + RMS_EPS) * g


def _resident(shape):
    nd = len(shape)
    return pl.BlockSpec(shape, lambda *_: (0,) * nd, pipeline_mode=pl.Buffered(1))


def _rope(x, tab_ref):
    return (x * tab_ref[0]
            + pltpu.roll(x, MLA_ROPE // 2, 1) * tab_ref[1]
            + pltpu.roll(x, LANES - MLA_ROPE // 2, 1) * tab_ref[2])


def _even_proj_body(h_ref, gmix_ref, wcq_ref, wckv_ref, wkr_ref, wb_ref, gcq_ref, wuq_ref,
                    gckv_ref, wuk_ref, wuv_ref, ropeq_ref, ropek_ref,
                    qa_ref, ka_ref, va_ref, qb_ref, kb_ref, vb_ref):
    u = _rms(h_ref[...], gmix_ref[...]).astype(BF16)
    nb = SB_HEADS * SB_DIM
    b = _dot(u, wb_ref[...])
    qb_ref[...] = (b[:, :nb] * (SB_DIM ** -0.5)).astype(BF16)
    kb_ref[...] = b[:, nb:2 * nb].astype(BF16)
    vb_ref[...] = b[:, 2 * nb:].astype(BF16)

    cqn = _rms(_dot(u, wcq_ref[...]), gcq_ref[...]).astype(BF16)
    ckvn = _rms(_dot(u, wckv_ref[...]), gckv_ref[...]).astype(BF16)
    va = _dot(ckvn, wuv_ref[...])
    ones_lane = lax.broadcasted_iota(jnp.int32, va.shape, 1) % LANES == MLA_V
    va_ref[...] = jnp.where(ones_lane, 1.0, va).astype(BF16)

    k_rope = _rope(_dot(u, wkr_ref[...]), ropek_ref)
    qa = _dot(cqn, wuq_ref[...])
    kn = _dot(ckvn, wuk_ref[...])
    for hh in range(MLA_HEADS):
        sl = slice(hh * LANES, (hh + 1) * LANES)
        qa_ref[:, sl] = _rope(qa[:, sl], ropeq_ref).astype(BF16)
        ka_ref[:, sl] = (kn[:, sl] + k_rope).astype(BF16)


def _even_proj(h, gmix, wcq, wckv, wkr, wb, gcq, wuq, gckv, wuk, wuv, ropeq, ropek, seq):
    t, d = h.shape
    tm = ROW_TILE
    per_seq = seq // tm
    row = lambda n: pl.BlockSpec((tm, n), lambda i: (i, 0))
    rope_spec = pl.BlockSpec((3, tm, LANES), lambda i: (0, i % per_seq, 0))
    weights = (wcq, wckv, wkr, wb, gcq, wuq, gckv, wuk, wuv)
    out_widths = (MLA_HEADS * LANES,) * 3 + (SB_HEADS * SB_DIM,) * 3
    nbytes = (sum(w.size * w.dtype.itemsize for w in weights)
              + 2 * tm * d * 4 + 2 * tm * sum(out_widths) * 2 + tm * 8192 * 4)
    return pl.pallas_call(
        _even_proj_body,
        grid=(t // tm,),
        in_specs=[row(d), _resident(gmix.shape)]
                 + [_resident(wcq.shape), _resident(wckv.shape), _resident(wkr.shape), _resident(wb.shape),
                    _resident(gcq.shape), _resident(wuq.shape), _resident(gckv.shape), _resident(wuk.shape),
                    _resident(wuv.shape), rope_spec, rope_spec],
        out_specs=[row(n) for n in out_widths],
        out_shape=[jax.ShapeDtypeStruct((t, n), BF16) for n in out_widths],
        compiler_params=_params(1, nbytes),
        name="even_proj",
    )(h, gmix, wcq, wckv, wkr, wb, gcq, wuq, gckv, wuk, wuv, ropeq, ropek)


def _head_select(shape, head_in_pair):
    lane = lax.broadcasted_iota(jnp.int32, shape, 1)
    return (lane < SB_DIM) if head_in_pair == 0 else (lane >= SB_DIM)


def _mla_body(q_ref, k_ref, v_ref, o_ref):
    tq, tk, HEADS_PER_STEP = Q_TILE, MLA_K_TILE, MLA_HEADS_PER_STEP
    qi = pl.program_id(2)
    per = tq // tk
    row_chunk = lax.broadcasted_iota(jnp.int32, (tq, tk), 0) // CHUNK
    col_chunk = lax.broadcasted_iota(jnp.int32, (tq, tk), 1) // CHUNK
    slots = [slice(hh * LANES, (hh + 1) * LANES) for hh in range(HEADS_PER_STEP)]
    qs = [q_ref[0, :, sl] for sl in slots]

    def tile(kb, carry, visible):
        rows = pl.ds(pl.multiple_of(kb * tk, tk), tk)
        heads = range(HEADS_PER_STEP)
        scores = [_dot_nt(qs[hh], k_ref[0, rows, slots[hh]]) for hh in heads]
        new = []
        for hh in heads:
            m, acc = carry[hh]
            s = scores[hh] if visible is None else jnp.where(visible, scores[hh], -jnp.inf)
            m_new = jnp.maximum(m, jnp.max(s, axis=-1, keepdims=True))
            p = jnp.exp2(s - m_new).astype(BF16)
            new.append((m_new, jnp.exp2(m - m_new) * acc, p))
        return tuple((m_new, acc + _dot(p, v_ref[0, rows, slots[hh]]))
                     for hh, (m_new, acc, p) in zip(heads, new))

    init = ((jnp.full((tq, 1), NEG_BIG, F32), jnp.zeros((tq, LANES), F32)),) * HEADS_PER_STEP
    carry = lax.fori_loop(0, per * qi, lambda kb, c: tile(kb, c, None), init)
    for d in range(per):
        carry = tile(per * qi + d, carry, col_chunk + d * (tk // CHUNK) <= row_chunk)
    outs = [acc / acc[:, MLA_V:MLA_V + 1] for _, acc in carry]
    lane = lax.broadcasted_iota(jnp.int32, (tq, LANES), 1)
    for p in range(HEADS_PER_STEP // 2):
        packed = jnp.where(lane < MLA_V, outs[2 * p], pltpu.roll(outs[2 * p + 1], MLA_V, 1))
        o_ref[0, :, p * LANES:(p + 1) * LANES] = packed.astype(BF16)


def _mla_attention(qa, ka, va):
    bsz, seq, _ = qa.shape
    tq, g = Q_TILE, MLA_HEADS_PER_STEP
    nbytes = 2 * (2 * seq * g * LANES * 2 + tq * g * LANES * 3) + 8 * g * tq * MLA_K_TILE * 4
    return pl.pallas_call(
        _mla_body,
        grid=(bsz, MLA_HEADS // g, seq // tq),
        in_specs=[pl.BlockSpec((1, tq, g * LANES), lambda b, hg, qi: (b, qi, hg)),
                  pl.BlockSpec((1, seq, g * LANES), lambda b, hg, qi: (b, 0, hg)),
                  pl.BlockSpec((1, seq, g * LANES), lambda b, hg, qi: (b, 0, hg))],
        out_specs=pl.BlockSpec((1, tq, g * MLA_V), lambda b, hg, qi: (b, qi, hg)),
        out_shape=jax.ShapeDtypeStruct((bsz, seq, MLA_HEADS * MLA_V), BF16),
        compiler_params=_params(3, nbytes),
        name="mla_attention",
    )(qa, ka, va)


def _sb_body(q_ref, k_ref, v_ref, ntri_ref, o_ref):
    tq, tk = Q_TILE, K_TILE
    qi = pl.program_id(2)
    per = tq // tk
    row = lax.broadcasted_iota(jnp.int32, (tq, tk), 0)
    col = lax.broadcasted_iota(jnp.int32, (tq, tk), 1)
    pair_slots = [slice(p * LANES, (p + 1) * LANES) for p in range(HEADS_PER_STEP // 2)]
    qs = []
    for hh in range(HEADS_PER_STEP):
        q_pair = q_ref[0, :, pair_slots[hh // 2]]
        qs.append(jnp.where(_head_select((tq, LANES), hh % 2), q_pair, jnp.zeros_like(q_pair)))

    def tile(kb, carry, before):
        rows = pl.ds(pl.multiple_of(kb * tk, tk), tk)
        heads = range(HEADS_PER_STEP)
        zs = [_dot_nt(qs[hh], k_ref[0, rows, pair_slots[hh // 2]]) for hh in heads]
        softplus = []
        for hh in heads:
            sp = jnp.maximum(zs[hh], 0.0) + jnp.log(1.0 + jnp.exp2(jnp.abs(zs[hh]) * -LOG2E))
            softplus.append(sp if before is None else jnp.where(before, sp, 0.0))
        cums = [_dot(softplus[hh].astype(BF16), ntri_ref[...]) + carry[hh][0] for hh in heads]
        weights = []
        for hh in heads:
            w = jnp.exp((zs[hh] - softplus[hh]) + cums[hh])
            if before is not None:
                w = jnp.where(before, w, 0.0)
            weights.append(w.astype(BF16))
        return tuple((cums[hh][:, 0:1] - softplus[hh][:, 0:1],
                      carry[hh][1] + _dot(weights[hh], v_ref[0, rows, pair_slots[hh // 2]]))
                     for hh in heads)

    carry = ((jnp.zeros((tq, 1), F32), jnp.zeros((tq, LANES), F32)),) * HEADS_PER_STEP
    for d in reversed(range(per)):
        carry = tile(per * qi + d, carry, col + d * tk < row)
    carry = lax.fori_loop(0, per * qi, lambda i, c: tile(per * qi - 1 - i, c, None), carry)
    for p in range(HEADS_PER_STEP // 2):
        packed = jnp.where(_head_select((tq, LANES), 0), carry[2 * p][1], carry[2 * p + 1][1])
        o_ref[0, :, pair_slots[p]] = packed.astype(BF16)


def _sb_attention(qb, kb, vb):
    bsz, seq, _ = qb.shape
    tq, tk, g = Q_TILE, K_TILE, HEADS_PER_STEP
    width = g * SB_DIM
    j = jnp.arange(tk)
    ntri2 = -(j[:, None] > j[None, :]).astype(BF16)
    nbytes = 2 * (2 * seq * width * 2 + 2 * tq * width * 2) + tk * tk * 2 + 8 * g * tq * tk * 4
    return pl.pallas_call(
        _sb_body,
        grid=(bsz, SB_HEADS // g, seq // tq),
        in_specs=[pl.BlockSpec((1, tq, width), lambda b, hg, qi: (b, qi, hg)),
                  pl.BlockSpec((1, seq, width), lambda b, hg, qi: (b, 0, hg)),
                  pl.BlockSpec((1, seq, width), lambda b, hg, qi: (b, 0, hg)),
                  _resident(ntri2.shape)],
        out_specs=pl.BlockSpec((1, tq, width), lambda b, hg, qi: (b, qi, hg)),
        out_shape=jax.ShapeDtypeStruct((bsz, seq, SB_HEADS * SB_DIM), BF16),
        compiler_params=_params(3, nbytes),
        name="sb_attention",
    )(qb, kb, vb, ntri2)


def _odd_proj_body(h_ref, gmix_ref, w_ref, q_ref, k_ref, v_ref):
    u = _rms(h_ref[...], gmix_ref[...]).astype(BF16)
    n = C_HEADS * C_DIM
    qkv = _dot(u, w_ref[...])
    q_ref[...] = (qkv[:, :n] * (C_DIM ** -0.5 * LOG2E)).astype(BF16)
    k_ref[...] = qkv[:, n:2 * n].astype(BF16)
    v_ref[...] = qkv[:, 2 * n:].astype(BF16)


def _odd_proj(h, gmix, w_qkv):
    t, d = h.shape
    tm = ROW_TILE
    n = C_HEADS * C_DIM
    row = lambda m: pl.BlockSpec((tm, m), lambda i: (i, 0))
    nbytes = w_qkv.size * 2 + 2 * tm * d * 4 + 2 * 3 * tm * n * 2 + 2 * tm * 3 * n * 4
    return pl.pallas_call(
        _odd_proj_body,
        grid=(t // tm,),
        in_specs=[row(d), _resident(gmix.shape), _resident(w_qkv.shape)],
        out_specs=[row(n)] * 3,
        out_shape=[jax.ShapeDtypeStruct((t, n), BF16)] * 3,
        compiler_params=_params(1, nbytes),
        name="odd_proj",
    )(h, gmix, w_qkv)


def _band_body(q_ref, k_ref, v_ref, bias_ref, o_ref):
    tq = ATT_TILE

    def group(q0, k0, n_keys, bias_off):
        heads = range(BAND_HEADS_PER_STEP)
        pair_slots = [slice(p * LANES, (p + 1) * LANES) for p in range(BAND_HEADS_PER_STEP // 2)]
        scores = []
        for hh in heads:
            q_pair = q_ref[0, pl.ds(q0, tq), pair_slots[hh // 2]]
            q = jnp.where(_head_select((tq, LANES), hh % 2), q_pair, jnp.zeros_like(q_pair))
            scores.append(_dot_nt(q, k_ref[0, pl.ds(k0, n_keys), pair_slots[hh // 2]]))
        probs, denoms = [], []
        for hh in heads:
            s = scores[hh] + bias_ref[hh, :, bias_off:bias_off + n_keys]
            p = jnp.exp2(s - jnp.max(s, axis=-1, keepdims=True))
            denoms.append(jnp.sum(p, axis=-1, keepdims=True))
            probs.append(p.astype(BF16))
        outs = [_dot(probs[hh], v_ref[0, pl.ds(k0, n_keys), pair_slots[hh // 2]]) / denoms[hh] for hh in heads]
        for p in range(BAND_HEADS_PER_STEP // 2):
            packed = jnp.where(_head_select((tq, LANES), 0), outs[2 * p], outs[2 * p + 1])
            o_ref[0, pl.ds(q0, tq), pair_slots[p]] = packed.astype(BF16)

    lead = LEFT_CHUNKS // BAND_GROUP
    for g in range(lead):
        group(g * tq, 0, (g + 1) * tq, LEFT_CHUNKS * CHUNK - g * tq)

    def body(g, _):
        q0 = pl.multiple_of(g * tq, tq)
        group(q0, pl.multiple_of(q0 - LEFT_CHUNKS * CHUNK, tq), BAND_KEYS, 0)
        return 0

    lax.fori_loop(lead, q_ref.shape[1] // tq, body, 0)


def _band_bias_table(rel_bias):
    tq, n_heads = ATT_TILE, rel_bias.shape[0]
    shift = LEFT_CHUNKS * CHUNK
    assert shift - (BAND_KEYS - 1) >= -REL_CLIP
    n_rel = tq - 1 + BAND_KEYS
    n_clipped = n_rel - (tq - 1 + REL_CLIP + 1)
    by_rel = jnp.concatenate(
        [rel_bias[:, REL_CLIP - (tq - 1):], jnp.broadcast_to(rel_bias[:, -1:], (n_heads, n_clipped))], axis=1)
    period = jnp.pad(by_rel[:, ::-1], ((0, 0), (0, 1)))
    flat = jnp.tile(period, (1, tq))[:, :tq * n_rel]
    skewed = flat.reshape(n_heads, tq, n_rel)
    start = (n_rel - 1) - (shift + tq - 1)
    bias = skewed[:, :, start:start + BAND_KEYS]
    q_chunk = jnp.arange(tq)[:, None] // CHUNK
    k_chunk = jnp.arange(BAND_KEYS)[None, :] // CHUNK
    in_band = (k_chunk >= q_chunk) & (k_chunk <= q_chunk + LEFT_CHUNKS)
    return jnp.where(in_band[None], bias * LOG2E, -jnp.inf).astype(F32)


def _band_attention(q, k, v, bias_table):
    bsz, seq, _ = q.shape
    g = BAND_HEADS_PER_STEP
    tq = ATT_TILE
    width = g * C_DIM
    seq_spec = pl.BlockSpec((1, seq, width), lambda b, hg: (b, 0, hg))
    nbytes = 2 * (4 * seq * width * 2 + g * tq * BAND_KEYS * 4) + 4 * g * tq * BAND_KEYS * 4
    return pl.pallas_call(
        _band_body,
        grid=(bsz, C_HEADS // g),
        in_specs=[seq_spec, seq_spec, seq_spec,
                  pl.BlockSpec((g, tq, BAND_KEYS), lambda b, hg: (hg, 0, 0))],
        out_specs=seq_spec,
        out_shape=jax.ShapeDtypeStruct((bsz, seq, C_HEADS * C_DIM), BF16),
        compiler_params=_params(2, nbytes),
        name="band_attention",
    )(q, k, v, bias_table)


def _out_proj_body(*refs, n_in):
    h_ref, g_ref = refs[0], refs[1]
    o_refs = refs[2:2 + n_in]
    w_refs = refs[2 + n_in:2 + 2 * n_in]
    hout_ref, u_ref = refs[2 + 2 * n_in:]
    h = h_ref[...]
    for o_ref, w_ref in zip(o_refs, w_refs):
        h = h + _dot(o_ref[...], w_ref[...])
    hout_ref[...] = h
    u_ref[...] = _rms(h, g_ref[...]).astype(BF16)


def _out_proj(h, g_ffn, outs, weights):
    t, d = h.shape
    tm = ROW_TILE
    row = lambda n: pl.BlockSpec((tm, n), lambda i: (i, 0))
    nbytes = (sum(w.size * 2 for w in weights) + 2 * tm * d * (4 + 4 + 2)
              + 2 * tm * sum(o.shape[1] for o in outs) * 2 + 2 * tm * d * 4)
    return pl.pallas_call(
        functools.partial(_out_proj_body, n_in=len(outs)),
        grid=(t // tm,),
        in_specs=[row(d), _resident(g_ffn.shape)] + [row(o.shape[1]) for o in outs]
                 + [_resident(w.shape) for w in weights],
        out_specs=[row(d), row(d)],
        out_shape=[jax.ShapeDtypeStruct((t, d), F32), jax.ShapeDtypeStruct((t, d), BF16)],
        compiler_params=_params(1, nbytes),
        name="out_proj",
    )(h, g_ffn, *outs, *weights)


def _ffn_body(u_ref, h_ref, wg_ref, wu_ref, wd_ref, gfin_ref, o_ref, acc_ref, *, final_norm):
    u = u_ref[...]
    d_ff = wg_ref.shape[1]
    acc_ref[...] = h_ref[...]
    for f0 in range(0, d_ff, FFN_COL_TILE):
        cols = slice(f0, f0 + FFN_COL_TILE)
        gate = _dot(u, wg_ref[:, cols])
        act = (gate * jax.nn.sigmoid(gate) * _dot(u, wu_ref[:, cols])).astype(BF16)
        acc_ref[...] += _dot(act, wd_ref[cols, :])
    out = acc_ref[...]
    o_ref[...] = _rms(out, gfin_ref[...]) if final_norm else out


def _ffn(u, h, wg, wu, wd, g_final, final_norm):
    t, d = h.shape
    tm = FFN_ROW_TILE
    d_ff = wg.shape[1]
    assert d_ff % FFN_COL_TILE == 0
    row = pl.BlockSpec((tm, d), lambda i: (i, 0))
    nbytes = 3 * d * d_ff * 2 + 2 * tm * d * (2 + 4 + 4) + tm * d * 4 + 4 * tm * FFN_COL_TILE * 4
    return pl.pallas_call(
        functools.partial(_ffn_body, final_norm=final_norm),
        grid=(t // tm,),
        in_specs=[row, row, _resident(wg.shape), _resident(wu.shape), _resident(wd.shape),
                  _resident(g_final.shape)],
        out_specs=row,
        out_shape=jax.ShapeDtypeStruct((t, d), F32),
        scratch_shapes=[pltpu.VMEM((tm, d), F32)],
        compiler_params=_params(1, nbytes),
        name="swiglu",
    )(u, h, wg, wu, wd, g_final)


def _head_slots(w, n_heads, per_head, start, keep):
    k = w.shape[0]
    w = w.reshape(k, n_heads, per_head)[:, :, start:start + keep]
    return jnp.pad(w, ((0, 0), (0, 0), (0, LANES - keep))).reshape(k, n_heads * LANES)


def _rope_slot_tables(seq, scale):
    half = MLA_ROPE // 2
    pos = jnp.arange(seq, dtype=F32)
    inv_freq = ROPE_THETA ** (-jnp.arange(0, MLA_ROPE, 2, dtype=F32) / MLA_ROPE)
    ang = pos[:, None] * inv_freq[None, :]
    cos, sin = jnp.cos(ang), jnp.sin(ang)
    zeros = jnp.zeros((seq, half), F32)
    pad = jnp.zeros((seq, LANES - MLA_NOPE - MLA_ROPE), F32)
    ones = jnp.ones((seq, MLA_NOPE), F32)
    direct = jnp.concatenate([ones, cos, cos, pad], axis=1)
    from_lower = jnp.concatenate([0 * ones, zeros, sin, pad], axis=1)
    from_upper = jnp.concatenate([0 * ones, -sin, zeros, pad], axis=1)
    return jnp.stack([direct, from_lower, from_upper]) * scale


def kernel(x, ev_w_in, ev_g_cq, ev_w_uq, ev_g_ckv, ev_w_ukv, ev_w_out, od_w_qkv, od_rel_bias, od_w_out,
           g_mix, g_ffn, w_gate, w_up, w_down, g_final):
    bsz, seq, d = x.shape
    depth = g_mix.shape[0]
    t = bsz * seq
    h = x.reshape(t, d)
    g_fin = g_final.reshape(1, d)
    rope_q = _rope_slot_tables(seq, (MLA_NOPE + MLA_ROPE) ** -0.5 * LOG2E)
    rope_k = _rope_slot_tables(seq, 1.0)

    for layer in range(depth):
        i = layer // 2
        gm = g_mix[layer].reshape(1, d)
        if layer % 2 == 0:
            w_in = ev_w_in[i]
            o1, o2, o3 = Q_LORA, Q_LORA + KV_LORA, Q_LORA + KV_LORA + MLA_ROPE
            wcq = w_in[:, :o1].astype(BF16)
            wckv = w_in[:, o1:o2].astype(BF16)
            wkr = jnp.pad(w_in[:, o2:o3], ((0, 0), (MLA_NOPE, LANES - MLA_NOPE - MLA_ROPE))).astype(BF16)
            wb = w_in[:, o3:].astype(BF16)
            wuq = _head_slots(ev_w_uq[i], MLA_HEADS, MLA_NOPE + MLA_ROPE, 0, MLA_NOPE + MLA_ROPE).astype(BF16)
            wuk = _head_slots(ev_w_ukv[i], MLA_HEADS, MLA_NOPE + MLA_V, 0, MLA_NOPE).astype(BF16)
            wuv = _head_slots(ev_w_ukv[i], MLA_HEADS, MLA_NOPE + MLA_V, MLA_NOPE, MLA_V).astype(BF16)
            qa, ka, va, qb, kb, vb = _even_proj(
                h, gm, wcq, wckv, wkr, wb, ev_g_cq[i].reshape(1, -1), wuq, ev_g_ckv[i].reshape(1, -1),
                wuk, wuv, rope_q, rope_k, seq)
            r3 = lambda a: a.reshape(bsz, seq, a.shape[1])
            o_a = _mla_attention(r3(qa), r3(ka), r3(va)).reshape(t, -1)
            o_b = _sb_attention(r3(qb), r3(kb), r3(vb)).reshape(t, -1)
            w_out = ev_w_out[i].astype(BF16)
            na = MLA_HEADS * MLA_V
            mix_outs, mix_w = [o_a, o_b], [w_out[:na], w_out[na:]]
        else:
            q, k, v = _odd_proj(h, gm, od_w_qkv[i].astype(BF16))
            r3 = lambda a: a.reshape(bsz, seq, a.shape[1])
            o_c = _band_attention(r3(q), r3(k), r3(v), _band_bias_table(od_rel_bias[i])).reshape(t, -1)
            mix_outs, mix_w = [o_c], [od_w_out[i].astype(BF16)]
        h, u = _out_proj(h, g_ffn[layer].reshape(1, d), mix_outs, mix_w)
        h = _ffn(u, h, w_gate[layer].astype(BF16), w_up[layer].astype(BF16), w_down[layer].astype(BF16),
                 g_fin, final_norm=(layer == depth - 1))
    return h.reshape(bsz, seq, d)
```

```python
import functools

import jax
import jax.numpy as jnp
from jax import lax
from jax.experimental import pallas as pl
from jax.experimental.pallas import tpu as pltpu

F32 = jnp.float32
BF16 = jnp.bfloat16

CHUNK = 64
MLA_HEADS = 8
MLA_NOPE = 64
MLA_ROPE = 32
MLA_V = 64
Q_LORA = 384
KV_LORA = 256
ROPE_THETA = 10000.0
SB_HEADS = 8
SB_DIM = 64
C_HEADS = 16
C_DIM = 64
LEFT_CHUNKS = 8
REL_CLIP = 256
RMS_EPS = 1e-6

LANES = 128
V7X_VMEM_BYTES = 64 * 1024 * 1024

ROW_TILE = 512
FFN_ROW_TILE = 512
FFN_COL_TILE = 256
Q_TILE = 512
K_TILE = 256
HEADS_PER_STEP = 4
MLA_K_TILE = 512
MLA_DIAG_TILE = 256
MLA_HEADS_PER_STEP = 4
BAND_HEADS_PER_STEP = 4
ATT_TILE = 256
BAND_GROUP = ATT_TILE // CHUNK
BAND_KEYS = (LEFT_CHUNKS + BAND_GROUP) * CHUNK

NEG_BIG = -1e30
LOG2E = 1.4426950408889634


def _vmem_limit(nbytes):
    return int(min(max(2 * nbytes, 32 * 1024 * 1024), V7X_VMEM_BYTES - 8 * 1024 * 1024))


def _params(n_axes, nbytes):
    return pltpu.CompilerParams(
        dimension_semantics=("arbitrary",) * n_axes,
        vmem_limit_bytes=_vmem_limit(nbytes),
    )


def _dot(a, b):
    return jnp.dot(a, b, preferred_element_type=F32)


def _dot_nt(a, b):
    return lax.dot_general(a, b, (((1,), (1,)), ((), ())), preferred_element_type=F32)


def _rms(x, g):
    return x * lax.rsqrt(jnp.mean(x * x, axis=-1, keepdims=True) + RMS_EPS) * g


def _resident(shape):
    nd = len(shape)
    return pl.BlockSpec(shape, lambda *_: (0,) * nd, pipeline_mode=pl.Buffered(1))


def _rope(x, tab_ref):
    return (x * tab_ref[0]
            + pltpu.roll(x, MLA_ROPE // 2, 1) * tab_ref[1]
            + pltpu.roll(x, LANES - MLA_ROPE // 2, 1) * tab_ref[2])


def _even_proj_body(h_ref, gmix_ref, wcq_ref, wckv_ref, wkr_ref, wb_ref, gcq_ref, wuq_ref,
                    gckv_ref, wuk_ref, wuv_ref, ropeq_ref, ropek_ref,
                    qa_ref, ka_ref, va_ref, qb_ref, kb_ref, vb_ref):
    u = _rms(h_ref[...], gmix_ref[...]).astype(BF16)
    nb = SB_HEADS * SB_DIM
    b = _dot(u, wb_ref[...])
    qb_ref[...] = (b[:, :nb] * (SB_DIM ** -0.5)).astype(BF16)
    kb_ref[...] = b[:, nb:2 * nb].astype(BF16)
    vb_ref[...] = b[:, 2 * nb:].astype(BF16)

    cqn = _rms(_dot(u, wcq_ref[...]), gcq_ref[...]).astype(BF16)
    ckvn = _rms(_dot(u, wckv_ref[...]), gckv_ref[...]).astype(BF16)
    va = _dot(ckvn, wuv_ref[...])
    ones_lane = lax.broadcasted_iota(jnp.int32, va.shape, 1) % LANES == MLA_V
    va_ref[...] = jnp.where(ones_lane, 1.0, va).astype(BF16)

    k_rope = _rope(_dot(u, wkr_ref[...]), ropek_ref)
    qa = _dot(cqn, wuq_ref[...])
    kn = _dot(ckvn, wuk_ref[...])
    for hh in range(MLA_HEADS):
        sl = slice(hh * LANES, (hh + 1) * LANES)
        qa_ref[:, sl] = _rope(qa[:, sl], ropeq_ref).astype(BF16)
        ka_ref[:, sl] = (kn[:, sl] + k_rope).astype(BF16)


def _even_proj(h, gmix, wcq, wckv, wkr, wb, gcq, wuq, gckv, wuk, wuv, ropeq, ropek, seq):
    t, d = h.shape
    tm = ROW_TILE
    per_seq = seq // tm
    row = lambda n: pl.BlockSpec((tm, n), lambda i: (i, 0))
    rope_spec = pl.BlockSpec((3, tm, LANES), lambda i: (0, i % per_seq, 0))
    weights = (wcq, wckv, wkr, wb, gcq, wuq, gckv, wuk, wuv)
    out_widths = (MLA_HEADS * LANES,) * 3 + (SB_HEADS * SB_DIM,) * 3
    nbytes = (sum(w.size * w.dtype.itemsize for w in weights)
              + 2 * tm * d * 4 + 2 * tm * sum(out_widths) * 2 + tm * 8192 * 4)
    return pl.pallas_call(
        _even_proj_body,
        grid=(t // tm,),
        in_specs=[row(d), _resident(gmix.shape)]
                 + [_resident(wcq.shape), _resident(wckv.shape), _resident(wkr.shape), _resident(wb.shape),
                    _resident(gcq.shape), _resident(wuq.shape), _resident(gckv.shape), _resident(wuk.shape),
                    _resident(wuv.shape), rope_spec, rope_spec],
        out_specs=[row(n) for n in out_widths],
        out_shape=[jax.ShapeDtypeStruct((t, n), BF16) for n in out_widths],
        compiler_params=_params(1, nbytes),
        name="even_proj",
    )(h, gmix, wcq, wckv, wkr, wb, gcq, wuq, gckv, wuk, wuv, ropeq, ropek)


def _head_select(shape, head_in_pair):
    lane = lax.broadcasted_iota(jnp.int32, shape, 1)
    return (lane < SB_DIM) if head_in_pair == 0 else (lane >= SB_DIM)


def _mla_body(q_ref, k_ref, v_ref, o_ref):
    tq, tk, HEADS_PER_STEP = Q_TILE, MLA_K_TILE, MLA_HEADS_PER_STEP
    assert tq == tk
    dk = MLA_DIAG_TILE
    qi = pl.program_id(2)
    row_chunk = lax.broadcasted_iota(jnp.int32, (tq, dk), 0) // CHUNK
    col_chunk = lax.broadcasted_iota(jnp.int32, (tq, dk), 1) // CHUNK
    slots = [slice(hh * LANES, (hh + 1) * LANES) for hh in range(HEADS_PER_STEP)]
    qs = [q_ref[0, :, sl] for sl in slots]

    def tile(k0, n_keys, carry, visible, first_row=0):
        rows = pl.ds(pl.multiple_of(k0, n_keys), n_keys)
        heads = range(HEADS_PER_STEP)
        scores = [_dot_nt(qs[hh][first_row:], k_ref[0, rows, slots[hh]]) for hh in heads]
        new = []
        for hh in heads:
            m, acc = carry[hh]
            s = scores[hh] if visible is None else jnp.where(visible, scores[hh], -jnp.inf)
            m_new = jnp.maximum(m, jnp.max(s, axis=-1, keepdims=True))
            p = jnp.exp2(s - m_new).astype(BF16)
            new.append((m_new, jnp.exp2(m - m_new) * acc, p))
        return tuple((m_new, acc + _dot(p, v_ref[0, rows, slots[hh]]))
                     for hh, (m_new, acc, p) in zip(heads, new))

    init = ((jnp.full((tq, 1), NEG_BIG, F32), jnp.zeros((tq, LANES), F32)),) * HEADS_PER_STEP
    carry = lax.fori_loop(0, qi, lambda kb, c: tile(kb * tk, tk, c, None), init)
    for d in range(tq // dk):
        first = d * dk
        part = tuple((m[first:], acc[first:]) for m, acc in carry)
        part = tile(qi * tq + first, dk, part, (col_chunk <= row_chunk)[:tq - first], first_row=first)
        carry = tuple((jnp.concatenate([m[:first], pm], axis=0), jnp.concatenate([acc[:first], pacc], axis=0))
                      for (m, acc), (pm, pacc) in zip(carry, part)) if first else part
    outs = [acc / acc[:, MLA_V:MLA_V + 1] for _, acc in carry]
    lane = lax.broadcasted_iota(jnp.int32, (tq, LANES), 1)
    for p in range(HEADS_PER_STEP // 2):
        packed = jnp.where(lane < MLA_V, outs[2 * p], pltpu.roll(outs[2 * p + 1], MLA_V, 1))
        o_ref[0, :, p * LANES:(p + 1) * LANES] = packed.astype(BF16)


def _mla_attention(qa, ka, va):
    bsz, seq, _ = qa.shape
    tq, g = Q_TILE, MLA_HEADS_PER_STEP
    nbytes = 2 * (2 * seq * g * LANES * 2 + tq * g * LANES * 3) + 8 * g * tq * MLA_K_TILE * 4
    return pl.pallas_call(
        _mla_body,
        grid=(bsz, MLA_HEADS // g, seq // tq),
        in_specs=[pl.BlockSpec((1, tq, g * LANES), lambda b, hg, qi: (b, qi, hg)),
                  pl.BlockSpec((1, seq, g * LANES), lambda b, hg, qi: (b, 0, hg)),
                  pl.BlockSpec((1, seq, g * LANES), lambda b, hg, qi: (b, 0, hg))],
        out_specs=pl.BlockSpec((1, tq, g * MLA_V), lambda b, hg, qi: (b, qi, hg)),
        out_shape=jax.ShapeDtypeStruct((bsz, seq, MLA_HEADS * MLA_V), BF16),
        compiler_params=_params(3, nbytes),
        name="mla_attention",
    )(qa, ka, va)


def _sb_body(q_ref, k_ref, v_ref, ntri_ref, o_ref):
    tq, tk = Q_TILE, K_TILE
    qi = pl.program_id(2)
    per = tq // tk
    row = lax.broadcasted_iota(jnp.int32, (tq, tk), 0)
    col = lax.broadcasted_iota(jnp.int32, (tq, tk), 1)
    pair_slots = [slice(p * LANES, (p + 1) * LANES) for p in range(HEADS_PER_STEP // 2)]
    qs = []
    for hh in range(HEADS_PER_STEP):
        q_pair = q_ref[0, :, pair_slots[hh // 2]]
        qs.append(jnp.where(_head_select((tq, LANES), hh % 2), q_pair, jnp.zeros_like(q_pair)))

    def tile(kb, carry, before, first_row=0):
        rows = pl.ds(pl.multiple_of(kb * tk, tk), tk)
        heads = range(HEADS_PER_STEP)
        zs = [_dot_nt(qs[hh][first_row:], k_ref[0, rows, pair_slots[hh // 2]]) for hh in heads]
        softplus = []
        for hh in heads:
            sp = jnp.maximum(zs[hh], 0.0) + jnp.log(1.0 + jnp.exp2(jnp.abs(zs[hh]) * -LOG2E))
            softplus.append(sp if before is None else jnp.where(before, sp, 0.0))
        cums = [_dot(softplus[hh].astype(BF16), ntri_ref[...]) + carry[hh][0] for hh in heads]
        weights = []
        for hh in heads:
            w = jnp.exp((zs[hh] - softplus[hh]) + cums[hh])
            if before is not None:
                w = jnp.where(before, w, 0.0)
            weights.append(w.astype(BF16))
        return tuple((cums[hh][:, 0:1] - softplus[hh][:, 0:1],
                      carry[hh][1] + _dot(weights[hh], v_ref[0, rows, pair_slots[hh // 2]]))
                     for hh in heads)

    carry = ((jnp.zeros((tk, 1), F32), jnp.zeros((tk, LANES), F32)),) * HEADS_PER_STEP
    for d in reversed(range(per)):
        first = d * tk
        if d < per - 1:
            carry = tuple((jnp.concatenate([jnp.zeros((tk, 1), F32), later], axis=0),
                           jnp.concatenate([jnp.zeros((tk, LANES), F32), acc], axis=0)) for later, acc in carry)
        carry = tile(per * qi + d, carry, (col < row)[:tq - first], first_row=first)

    def earlier_tiles(i, c):
        for d in range(per):
            c = tile(per * (qi - i) - 1 - d, c, None)
        return c

    carry = lax.fori_loop(0, qi, earlier_tiles, carry)
    for p in range(HEADS_PER_STEP // 2):
        packed = jnp.where(_head_select((tq, LANES), 0), carry[2 * p][1], carry[2 * p + 1][1])
        o_ref[0, :, pair_slots[p]] = packed.astype(BF16)


def _sb_attention(qb, kb, vb):
    bsz, seq, _ = qb.shape
    tq, tk, g = Q_TILE, K_TILE, HEADS_PER_STEP
    width = g * SB_DIM
    j = jnp.arange(tk)
    ntri2 = -(j[:, None] > j[None, :]).astype(BF16)
    nbytes = 2 * (2 * seq * width * 2 + 2 * tq * width * 2) + tk * tk * 2 + 8 * g * tq * tk * 4
    return pl.pallas_call(
        _sb_body,
        grid=(bsz, SB_HEADS // g, seq // tq),
        in_specs=[pl.BlockSpec((1, tq, width), lambda b, hg, qi: (b, qi, hg)),
                  pl.BlockSpec((1, seq, width), lambda b, hg, qi: (b, 0, hg)),
                  pl.BlockSpec((1, seq, width), lambda b, hg, qi: (b, 0, hg)),
                  _resident(ntri2.shape)],
        out_specs=pl.BlockSpec((1, tq, width), lambda b, hg, qi: (b, qi, hg)),
        out_shape=jax.ShapeDtypeStruct((bsz, seq, SB_HEADS * SB_DIM), BF16),
        compiler_params=_params(3, nbytes),
        name="sb_attention",
    )(qb, kb, vb, ntri2)


def _odd_proj_body(h_ref, gmix_ref, w_ref, q_ref, k_ref, v_ref):
    u = _rms(h_ref[...], gmix_ref[...]).astype(BF16)
    n = C_HEADS * C_DIM
    qkv = _dot(u, w_ref[...])
    q_ref[...] = (qkv[:, :n] * (C_DIM ** -0.5 * LOG2E)).astype(BF16)
    k_ref[...] = qkv[:, n:2 * n].astype(BF16)
    v_ref[...] = qkv[:, 2 * n:].astype(BF16)


def _odd_proj(h, gmix, w_qkv):
    t, d = h.shape
    tm = ROW_TILE
    n = C_HEADS * C_DIM
    row = lambda m: pl.BlockSpec((tm, m), lambda i: (i, 0))
    nbytes = w_qkv.size * 2 + 2 * tm * d * 4 + 2 * 3 * tm * n * 2 + 2 * tm * 3 * n * 4
    return pl.pallas_call(
        _odd_proj_body,
        grid=(t // tm,),
        in_specs=[row(d), _resident(gmix.shape), _resident(w_qkv.shape)],
        out_specs=[row(n)] * 3,
        out_shape=[jax.ShapeDtypeStruct((t, n), BF16)] * 3,
        compiler_params=_params(1, nbytes),
        name="odd_proj",
    )(h, gmix, w_qkv)


def _band_body(q_ref, k_ref, v_ref, bias_ref, o_ref):
    tq = ATT_TILE

    def group(q0, k0, n_keys, bias_off):
        heads = range(BAND_HEADS_PER_STEP)
        pair_slots = [slice(p * LANES, (p + 1) * LANES) for p in range(BAND_HEADS_PER_STEP // 2)]
        scores = []
        for hh in heads:
            q_pair = q_ref[0, pl.ds(q0, tq), pair_slots[hh // 2]]
            q = jnp.where(_head_select((tq, LANES), hh % 2), q_pair, jnp.zeros_like(q_pair))
            scores.append(_dot_nt(q, k_ref[0, pl.ds(k0, n_keys), pair_slots[hh // 2]]))
        probs, denoms = [], []
        for hh in heads:
            s = scores[hh] + bias_ref[hh, :, bias_off:bias_off + n_keys]
            p = jnp.exp2(s - jnp.max(s, axis=-1, keepdims=True))
            denoms.append(jnp.sum(p, axis=-1, keepdims=True))
            probs.append(p.astype(BF16))
        outs = [_dot(probs[hh], v_ref[0, pl.ds(k0, n_keys), pair_slots[hh // 2]]) / denoms[hh] for hh in heads]
        for p in range(BAND_HEADS_PER_STEP // 2):
            packed = jnp.where(_head_select((tq, LANES), 0), outs[2 * p], outs[2 * p + 1])
            o_ref[0, pl.ds(q0, tq), pair_slots[p]] = packed.astype(BF16)

    lead = LEFT_CHUNKS // BAND_GROUP
    for g in range(lead):
        group(g * tq, 0, (g + 1) * tq, LEFT_CHUNKS * CHUNK - g * tq)

    def body(g, _):
        q0 = pl.multiple_of(g * tq, tq)
        group(q0, pl.multiple_of(q0 - LEFT_CHUNKS * CHUNK, tq), BAND_KEYS, 0)
        return 0

    lax.fori_loop(lead, q_ref.shape[1] // tq, body, 0)


def _band_bias_table(rel_bias):
    tq, n_heads = ATT_TILE, rel_bias.shape[0]
    shift = LEFT_CHUNKS * CHUNK
    assert shift - (BAND_KEYS - 1) >= -REL_CLIP
    n_rel = tq - 1 + BAND_KEYS
    n_clipped = n_rel - (tq - 1 + REL_CLIP + 1)
    by_rel = jnp.concatenate(
        [rel_bias[:, REL_CLIP - (tq - 1):], jnp.broadcast_to(rel_bias[:, -1:], (n_heads, n_clipped))], axis=1)
    period = jnp.pad(by_rel[:, ::-1], ((0, 0), (0, 1)))
    flat = jnp.tile(period, (1, tq))[:, :tq * n_rel]
    skewed = flat.reshape(n_heads, tq, n_rel)
    start = (n_rel - 1) - (shift + tq - 1)
    bias = skewed[:, :, start:start + BAND_KEYS]
    q_chunk = jnp.arange(tq)[:, None] // CHUNK
    k_chunk = jnp.arange(BAND_KEYS)[None, :] // CHUNK
    in_band = (k_chunk >= q_chunk) & (k_chunk <= q_chunk + LEFT_CHUNKS)
    return jnp.where(in_band[None], bias * LOG2E, -jnp.inf).astype(F32)


def _band_attention(q, k, v, bias_table):
    bsz, seq, _ = q.shape
    g = BAND_HEADS_PER_STEP
    tq = ATT_TILE
    width = g * C_DIM
    seq_spec = pl.BlockSpec((1, seq, width), lambda b, hg: (b, 0, hg))
    nbytes = 2 * (4 * seq * width * 2 + g * tq * BAND_KEYS * 4) + 4 * g * tq * BAND_KEYS * 4
    return pl.pallas_call(
        _band_body,
        grid=(bsz, C_HEADS // g),
        in_specs=[seq_spec, seq_spec, seq_spec,
                  pl.BlockSpec((g, tq, BAND_KEYS), lambda b, hg: (hg, 0, 0))],
        out_specs=seq_spec,
        out_shape=jax.ShapeDtypeStruct((bsz, seq, C_HEADS * C_DIM), BF16),
        compiler_params=_params(2, nbytes),
        name="band_attention",
    )(q, k, v, bias_table)


def _mix_ffn_body(*refs, n_mix, final_norm):
    h_ref, gffn_ref = refs[0], refs[1]
    mix_refs = refs[2:2 + n_mix]
    wo_ref, wg_ref, wu_ref, wd_ref, gfin_ref, o_ref, acc_ref = refs[2 + n_mix:]
    mixed = jnp.concatenate([r[...] for r in mix_refs], axis=1) if n_mix > 1 else mix_refs[0][...]
    h = h_ref[...] + _dot(mixed, wo_ref[...])
    u = _rms(h, gffn_ref[...]).astype(BF16)
    d_ff = wg_ref.shape[1]
    acc_ref[...] = h
    for f0 in range(0, d_ff, FFN_COL_TILE):
        cols = slice(f0, f0 + FFN_COL_TILE)
        gate = _dot(u, wg_ref[:, cols])
        act = (gate * jax.nn.sigmoid(gate) * _dot(u, wu_ref[:, cols])).astype(BF16)
        acc_ref[...] += _dot(act, wd_ref[cols, :])
    out = acc_ref[...]
    o_ref[...] = _rms(out, gfin_ref[...]) if final_norm else out


def _mix_ffn(h, g_ffn, mix_outs, w_out, wg, wu, wd, g_final, final_norm):
    t, d = h.shape
    tm = FFN_ROW_TILE
    d_ff = wg.shape[1]
    assert d_ff % FFN_COL_TILE == 0
    row = lambda n: pl.BlockSpec((tm, n), lambda i: (i, 0))
    nbytes = ((3 * d * d_ff + w_out.size) * 2 + 2 * tm * d * (4 + 4) + 2 * tm * w_out.shape[0] * 2
              + 3 * tm * d * 4 + 4 * tm * FFN_COL_TILE * 4)
    return pl.pallas_call(
        functools.partial(_mix_ffn_body, n_mix=len(mix_outs), final_norm=final_norm),
        grid=(t // tm,),
        in_specs=[row(d), _resident(g_ffn.shape)] + [row(o.shape[1]) for o in mix_outs]
                 + [_resident(w_out.shape), _resident(wg.shape), _resident(wu.shape), _resident(wd.shape),
                    _resident(g_final.shape)],
        out_specs=row(d),
        out_shape=jax.ShapeDtypeStruct((t, d), F32),
        scratch_shapes=[pltpu.VMEM((tm, d), F32)],
        compiler_params=_params(1, nbytes),
        name="mix_ffn",
    )(h, g_ffn, *mix_outs, w_out, wg, wu, wd, g_final)


def _head_slots(w, n_heads, per_head, start, keep):
    k = w.shape[0]
    w = w.reshape(k, n_heads, per_head)[:, :, start:start + keep]
    return jnp.pad(w, ((0, 0), (0, 0), (0, LANES - keep))).reshape(k, n_heads * LANES)


def _rope_slot_tables(seq, scale):
    half = MLA_ROPE // 2
    pos = jnp.arange(seq, dtype=F32)
    inv_freq = ROPE_THETA ** (-jnp.arange(0, MLA_ROPE, 2, dtype=F32) / MLA_ROPE)
    ang = pos[:, None] * inv_freq[None, :]
    cos, sin = jnp.cos(ang), jnp.sin(ang)
    zeros = jnp.zeros((seq, half), F32)
    pad = jnp.zeros((seq, LANES - MLA_NOPE - MLA_ROPE), F32)
    ones = jnp.ones((seq, MLA_NOPE), F32)
    direct = jnp.concatenate([ones, cos, cos, pad], axis=1)
    from_lower = jnp.concatenate([0 * ones, zeros, sin, pad], axis=1)
    from_upper = jnp.concatenate([0 * ones, -sin, zeros, pad], axis=1)
    return jnp.stack([direct, from_lower, from_upper]) * scale


def kernel(x, ev_w_in, ev_g_cq, ev_w_uq, ev_g_ckv, ev_w_ukv, ev_w_out, od_w_qkv, od_rel_bias, od_w_out,
           g_mix, g_ffn, w_gate, w_up, w_down, g_final):
    bsz, seq, d = x.shape
    depth = g_mix.shape[0]
    t = bsz * seq
    h = x.reshape(t, d)
    g_fin = g_final.reshape(1, d)
    rope_q = _rope_slot_tables(seq, (MLA_NOPE + MLA_ROPE) ** -0.5 * LOG2E)
    rope_k = _rope_slot_tables(seq, 1.0)

    for layer in range(depth):
        i = layer // 2
        gm = g_mix[layer].reshape(1, d)
        if layer % 2 == 0:
            w_in = ev_w_in[i]
            o1, o2, o3 = Q_LORA, Q_LORA + KV_LORA, Q_LORA + KV_LORA + MLA_ROPE
            wcq = w_in[:, :o1].astype(BF16)
            wckv = w_in[:, o1:o2].astype(BF16)
            wkr = jnp.pad(w_in[:, o2:o3], ((0, 0), (MLA_NOPE, LANES - MLA_NOPE - MLA_ROPE))).astype(BF16)
            wb = w_in[:, o3:].astype(BF16)
            wuq = _head_slots(ev_w_uq[i], MLA_HEADS, MLA_NOPE + MLA_ROPE, 0, MLA_NOPE + MLA_ROPE).astype(BF16)
            wuk = _head_slots(ev_w_ukv[i], MLA_HEADS, MLA_NOPE + MLA_V, 0, MLA_NOPE).astype(BF16)
            wuv = _head_slots(ev_w_ukv[i], MLA_HEADS, MLA_NOPE + MLA_V, MLA_NOPE, MLA_V).astype(BF16)
            qa, ka, va, qb, kb, vb = _even_proj(
                h, gm, wcq, wckv, wkr, wb, ev_g_cq[i].reshape(1, -1), wuq, ev_g_ckv[i].reshape(1, -1),
                wuk, wuv, rope_q, rope_k, seq)
            r3 = lambda a: a.reshape(bsz, seq, a.shape[1])
            o_a = _mla_attention(r3(qa), r3(ka), r3(va)).reshape(t, -1)
            o_b = _sb_attention(r3(qb), r3(kb), r3(vb)).reshape(t, -1)
            mix_outs, w_out = [o_a, o_b], ev_w_out[i].astype(BF16)
        else:
            q, k, v = _odd_proj(h, gm, od_w_qkv[i].astype(BF16))
            r3 = lambda a: a.reshape(bsz, seq, a.shape[1])
            o_c = _band_attention(r3(q), r3(k), r3(v), _band_bias_table(od_rel_bias[i])).reshape(t, -1)
            mix_outs, w_out = [o_c], od_w_out[i].astype(BF16)
        h = _mix_ffn(h, g_ffn[layer].reshape(1, d), mix_outs, w_out, w_gate[layer].astype(BF16),
                     w_up[layer].astype(BF16), w_down[layer].astype(BF16), g_fin,
                     final_norm=(layer == depth - 1))
    return h.reshape(bsz, seq, d)
```

```python
import functools

import jax
import jax.numpy as jnp
from jax import lax
from jax.experimental import pallas as pl
from jax.experimental.pallas import tpu as pltpu

F32 = jnp.float32
BF16 = jnp.bfloat16

CHUNK = 64
MLA_HEADS = 8
MLA_NOPE = 64
MLA_ROPE = 32
MLA_V = 64
Q_LORA = 384
KV_LORA = 256
ROPE_THETA = 10000.0
SB_HEADS = 8
SB_DIM = 64
C_HEADS = 16
C_DIM = 64
LEFT_CHUNKS = 8
REL_CLIP = 256
RMS_EPS = 1e-6

LANES = 128
V7X_VMEM_BYTES = 64 * 1024 * 1024

ROW_TILE = 512
FFN_ROW_TILE = 512
FFN_COL_TILE = 256
Q_TILE = 512
K_TILE = 256
HEADS_PER_STEP = 4
MLA_K_TILE = 512
MLA_DIAG_TILE = 256
MLA_HEADS_PER_STEP = 4
BAND_HEADS_PER_STEP = 4
ATT_TILE = 256
BAND_GROUP = ATT_TILE // CHUNK
BAND_KEYS = (LEFT_CHUNKS + BAND_GROUP) * CHUNK
WEIGHT_CAST_STEPS = 8

NEG_BIG = -1e30
LOG2E = 1.4426950408889634


def _vmem_limit(nbytes):
    return int(min(max(2 * nbytes, 32 * 1024 * 1024), V7X_VMEM_BYTES - 8 * 1024 * 1024))


def _params(n_axes, nbytes):
    return pltpu.CompilerParams(
        dimension_semantics=("arbitrary",) * n_axes,
        vmem_limit_bytes=_vmem_limit(nbytes),
    )


def _dot(a, b):
    return jnp.dot(a, b, preferred_element_type=F32)


def _dot_nt(a, b):
    return lax.dot_general(a, b, (((1,), (1,)), ((), ())), preferred_element_type=F32)


def _rms(x, g):
    return x * lax.rsqrt(jnp.mean(x * x, axis=-1, keepdims=True) + RMS_EPS) * g


def _resident(shape):
    nd = len(shape)
    return pl.BlockSpec(shape, lambda *_: (0,) * nd, pipeline_mode=pl.Buffered(1))


def _rope(x, tab_ref):
    return (x * tab_ref[0]
            + pltpu.roll(x, MLA_ROPE // 2, 1) * tab_ref[1]
            + pltpu.roll(x, LANES - MLA_ROPE // 2, 1) * tab_ref[2])


def _even_proj_body(h_ref, gmix_ref, wcq_ref, wckv_ref, wkr_ref, wb_ref, gcq_ref, wuq_ref,
                    gckv_ref, wuk_ref, wuv_ref, ropeq_ref, ropek_ref,
                    qa_ref, ka_ref, va_ref, qb_ref, kb_ref, vb_ref):
    u = _rms(h_ref[...], gmix_ref[...]).astype(BF16)
    nb = SB_HEADS * SB_DIM
    b = _dot(u, wb_ref[...])
    qb_ref[...] = (b[:, :nb] * (SB_DIM ** -0.5)).astype(BF16)
    kb_ref[...] = b[:, nb:2 * nb].astype(BF16)
    vb_ref[...] = b[:, 2 * nb:].astype(BF16)

    cqn = _rms(_dot(u, wcq_ref[...]), gcq_ref[...]).astype(BF16)
    ckvn = _rms(_dot(u, wckv_ref[...]), gckv_ref[...]).astype(BF16)
    va = _dot(ckvn, wuv_ref[...])
    ones_lane = lax.broadcasted_iota(jnp.int32, va.shape, 1) % LANES == MLA_V
    va_ref[...] = jnp.where(ones_lane, 1.0, va).astype(BF16)

    k_rope = _rope(_dot(u, wkr_ref[...]), ropek_ref)
    qa = _dot(cqn, wuq_ref[...])
    kn = _dot(ckvn, wuk_ref[...])
    for hh in range(MLA_HEADS):
        sl = slice(hh * LANES, (hh + 1) * LANES)
        qa_ref[:, sl] = _rope(qa[:, sl], ropeq_ref).astype(BF16)
        ka_ref[:, sl] = (kn[:, sl] + k_rope).astype(BF16)


def _even_proj(h, gmix, wcq, wckv, wkr, wb, gcq, wuq, gckv, wuk, wuv, ropeq, ropek, seq):
    t, d = h.shape
    tm = ROW_TILE
    per_seq = seq // tm
    row = lambda n: pl.BlockSpec((tm, n), lambda i: (i, 0))
    rope_spec = pl.BlockSpec((3, tm, LANES), lambda i: (0, i % per_seq, 0))
    weights = (wcq, wckv, wkr, wb, gcq, wuq, gckv, wuk, wuv)
    out_widths = (MLA_HEADS * LANES,) * 3 + (SB_HEADS * SB_DIM,) * 3
    nbytes = (sum(w.size * w.dtype.itemsize for w in weights)
              + 2 * tm * d * 4 + 2 * tm * sum(out_widths) * 2 + tm * 8192 * 4)
    return pl.pallas_call(
        _even_proj_body,
        grid=(t // tm,),
        in_specs=[row(d), _resident(gmix.shape)]
                 + [_resident(wcq.shape), _resident(wckv.shape), _resident(wkr.shape), _resident(wb.shape),
                    _resident(gcq.shape), _resident(wuq.shape), _resident(gckv.shape), _resident(wuk.shape),
                    _resident(wuv.shape), rope_spec, rope_spec],
        out_specs=[row(n) for n in out_widths],
        out_shape=[jax.ShapeDtypeStruct((t, n), BF16) for n in out_widths],
        compiler_params=_params(1, nbytes),
        name="even_proj",
    )(h, gmix, wcq, wckv, wkr, wb, gcq, wuq, gckv, wuk, wuv, ropeq, ropek)


def _head_select(shape, head_in_pair):
    lane = lax.broadcasted_iota(jnp.int32, shape, 1)
    return (lane < SB_DIM) if head_in_pair == 0 else (lane >= SB_DIM)


def _mla_body(q_ref, k_ref, v_ref, o_ref):
    tq, tk, HEADS_PER_STEP = Q_TILE, MLA_K_TILE, MLA_HEADS_PER_STEP
    assert tq == tk
    dk = MLA_DIAG_TILE
    qi = pl.program_id(2)
    row_chunk = lax.broadcasted_iota(jnp.int32, (tq, dk), 0) // CHUNK
    col_chunk = lax.broadcasted_iota(jnp.int32, (tq, dk), 1) // CHUNK
    slots = [slice(hh * LANES, (hh + 1) * LANES) for hh in range(HEADS_PER_STEP)]
    qs = [q_ref[0, :, sl] for sl in slots]

    def tile(k0, n_keys, carry, visible, first_row=0):
        rows = pl.ds(pl.multiple_of(k0, n_keys), n_keys)
        heads = range(HEADS_PER_STEP)
        scores = [_dot_nt(qs[hh][first_row:], k_ref[0, rows, slots[hh]]) for hh in heads]
        new = []
        for hh in heads:
            m, acc = carry[hh]
            s = scores[hh] if visible is None else jnp.where(visible, scores[hh], -jnp.inf)
            m_new = jnp.maximum(m, jnp.max(s, axis=-1, keepdims=True))
            p = jnp.exp2(s - m_new).astype(BF16)
            new.append((m_new, jnp.exp2(m - m_new) * acc, p))
        return tuple((m_new, acc + _dot(p, v_ref[0, rows, slots[hh]]))
                     for hh, (m_new, acc, p) in zip(heads, new))

    init = ((jnp.full((tq, 1), NEG_BIG, F32), jnp.zeros((tq, LANES), F32)),) * HEADS_PER_STEP
    carry = lax.fori_loop(0, qi, lambda kb, c: tile(kb * tk, tk, c, None), init)
    for d in range(tq // dk):
        first = d * dk
        part = tuple((m[first:], acc[first:]) for m, acc in carry)
        part = tile(qi * tq + first, dk, part, (col_chunk <= row_chunk)[:tq - first], first_row=first)
        carry = tuple((jnp.concatenate([m[:first], pm], axis=0), jnp.concatenate([acc[:first], pacc], axis=0))
                      for (m, acc), (pm, pacc) in zip(carry, part)) if first else part
    outs = [acc / acc[:, MLA_V:MLA_V + 1] for _, acc in carry]
    lane = lax.broadcasted_iota(jnp.int32, (tq, LANES), 1)
    for p in range(HEADS_PER_STEP // 2):
        packed = jnp.where(lane < MLA_V, outs[2 * p], pltpu.roll(outs[2 * p + 1], MLA_V, 1))
        o_ref[0, :, p * LANES:(p + 1) * LANES] = packed.astype(BF16)


def _mla_attention(qa, ka, va):
    bsz, seq, _ = qa.shape
    tq, g = Q_TILE, MLA_HEADS_PER_STEP
    nbytes = 2 * (2 * seq * g * LANES * 2 + tq * g * LANES * 3) + 8 * g * tq * MLA_K_TILE * 4
    return pl.pallas_call(
        _mla_body,
        grid=(bsz, MLA_HEADS // g, seq // tq),
        in_specs=[pl.BlockSpec((1, tq, g * LANES), lambda b, hg, qi: (b, qi, hg)),
                  pl.BlockSpec((1, seq, g * LANES), lambda b, hg, qi: (b, 0, hg)),
                  pl.BlockSpec((1, seq, g * LANES), lambda b, hg, qi: (b, 0, hg))],
        out_specs=pl.BlockSpec((1, tq, g * MLA_V), lambda b, hg, qi: (b, qi, hg)),
        out_shape=jax.ShapeDtypeStruct((bsz, seq, MLA_HEADS * MLA_V), BF16),
        compiler_params=_params(3, nbytes),
        name="mla_attention",
    )(qa, ka, va)


def _sb_body(q_ref, k_ref, v_ref, ntri_ref, o_ref):
    tq, tk = Q_TILE, K_TILE
    qi = pl.program_id(2)
    per = tq // tk
    row = lax.broadcasted_iota(jnp.int32, (tq, tk), 0)
    col = lax.broadcasted_iota(jnp.int32, (tq, tk), 1)
    pair_slots = [slice(p * LANES, (p + 1) * LANES) for p in range(HEADS_PER_STEP // 2)]
    qs = []
    for hh in range(HEADS_PER_STEP):
        q_pair = q_ref[0, :, pair_slots[hh // 2]]
        qs.append(jnp.where(_head_select((tq, LANES), hh % 2), q_pair, jnp.zeros_like(q_pair)))

    def tile(kb, carry, before, first_row=0):
        rows = pl.ds(pl.multiple_of(kb * tk, tk), tk)
        heads = range(HEADS_PER_STEP)
        zs = [_dot_nt(qs[hh][first_row:], k_ref[0, rows, pair_slots[hh // 2]]) for hh in heads]
        softplus = []
        for hh in heads:
            sp = jnp.maximum(zs[hh], 0.0) + jnp.log(1.0 + jnp.exp2(jnp.abs(zs[hh]) * -LOG2E))
            softplus.append(sp if before is None else jnp.where(before, sp, 0.0))
        cums = [_dot(softplus[hh].astype(BF16), ntri_ref[...]) + carry[hh][0] for hh in heads]
        weights = []
        for hh in heads:
            w = jnp.exp((zs[hh] - softplus[hh]) + cums[hh])
            if before is not None:
                w = jnp.where(before, w, 0.0)
            weights.append(w.astype(BF16))
        return tuple((cums[hh][:, 0:1] - softplus[hh][:, 0:1],
                      carry[hh][1] + _dot(weights[hh], v_ref[0, rows, pair_slots[hh // 2]]))
                     for hh in heads)

    carry = ((jnp.zeros((tk, 1), F32), jnp.zeros((tk, LANES), F32)),) * HEADS_PER_STEP
    for d in reversed(range(per)):
        first = d * tk
        if d < per - 1:
            carry = tuple((jnp.concatenate([jnp.zeros((tk, 1), F32), later], axis=0),
                           jnp.concatenate([jnp.zeros((tk, LANES), F32), acc], axis=0)) for later, acc in carry)
        carry = tile(per * qi + d, carry, (col < row)[:tq - first], first_row=first)

    def earlier_tiles(i, c):
        for d in range(per):
            c = tile(per * (qi - i) - 1 - d, c, None)
        return c

    carry = lax.fori_loop(0, qi, earlier_tiles, carry)
    for p in range(HEADS_PER_STEP // 2):
        packed = jnp.where(_head_select((tq, LANES), 0), carry[2 * p][1], carry[2 * p + 1][1])
        o_ref[0, :, pair_slots[p]] = packed.astype(BF16)


def _sb_attention(qb, kb, vb):
    bsz, seq, _ = qb.shape
    tq, tk, g = Q_TILE, K_TILE, HEADS_PER_STEP
    width = g * SB_DIM
    j = jnp.arange(tk)
    ntri2 = -(j[:, None] > j[None, :]).astype(BF16)
    nbytes = 2 * (2 * seq * width * 2 + 2 * tq * width * 2) + tk * tk * 2 + 8 * g * tq * tk * 4
    return pl.pallas_call(
        _sb_body,
        grid=(bsz, SB_HEADS // g, seq // tq),
        in_specs=[pl.BlockSpec((1, tq, width), lambda b, hg, qi: (b, qi, hg)),
                  pl.BlockSpec((1, seq, width), lambda b, hg, qi: (b, 0, hg)),
                  pl.BlockSpec((1, seq, width), lambda b, hg, qi: (b, 0, hg)),
                  _resident(ntri2.shape)],
        out_specs=pl.BlockSpec((1, tq, width), lambda b, hg, qi: (b, qi, hg)),
        out_shape=jax.ShapeDtypeStruct((bsz, seq, SB_HEADS * SB_DIM), BF16),
        compiler_params=_params(3, nbytes),
        name="sb_attention",
    )(qb, kb, vb, ntri2)


def _odd_proj_body(h_ref, gmix_ref, w_ref, q_ref, k_ref, v_ref):
    u = _rms(h_ref[...], gmix_ref[...]).astype(BF16)
    n = C_HEADS * C_DIM
    qkv = _dot(u, w_ref[...])
    q_ref[...] = (qkv[:, :n] * (C_DIM ** -0.5 * LOG2E)).astype(BF16)
    k_ref[...] = qkv[:, n:2 * n].astype(BF16)
    v_ref[...] = qkv[:, 2 * n:].astype(BF16)


def _odd_proj(h, gmix, w_qkv):
    t, d = h.shape
    tm = ROW_TILE
    n = C_HEADS * C_DIM
    row = lambda m: pl.BlockSpec((tm, m), lambda i: (i, 0))
    nbytes = w_qkv.size * 2 + 2 * tm * d * 4 + 2 * 3 * tm * n * 2 + 2 * tm * 3 * n * 4
    return pl.pallas_call(
        _odd_proj_body,
        grid=(t // tm,),
        in_specs=[row(d), _resident(gmix.shape), _resident(w_qkv.shape)],
        out_specs=[row(n)] * 3,
        out_shape=[jax.ShapeDtypeStruct((t, n), BF16)] * 3,
        compiler_params=_params(1, nbytes),
        name="odd_proj",
    )(h, gmix, w_qkv)


def _band_body(q_ref, k_ref, v_ref, bias_ref, o_ref):
    tq = ATT_TILE

    def group(q0, k0, n_keys, bias_off):
        heads = range(BAND_HEADS_PER_STEP)
        pair_slots = [slice(p * LANES, (p + 1) * LANES) for p in range(BAND_HEADS_PER_STEP // 2)]
        scores = []
        for hh in heads:
            q_pair = q_ref[0, pl.ds(q0, tq), pair_slots[hh // 2]]
            q = jnp.where(_head_select((tq, LANES), hh % 2), q_pair, jnp.zeros_like(q_pair))
            scores.append(_dot_nt(q, k_ref[0, pl.ds(k0, n_keys), pair_slots[hh // 2]]))
        probs, denoms = [], []
        for hh in heads:
            s = scores[hh] + bias_ref[hh, :, bias_off:bias_off + n_keys]
            p = jnp.exp2(s - jnp.max(s, axis=-1, keepdims=True))
            denoms.append(jnp.sum(p, axis=-1, keepdims=True))
            probs.append(p.astype(BF16))
        outs = [_dot(probs[hh], v_ref[0, pl.ds(k0, n_keys), pair_slots[hh // 2]]) / denoms[hh] for hh in heads]
        for p in range(BAND_HEADS_PER_STEP // 2):
            packed = jnp.where(_head_select((tq, LANES), 0), outs[2 * p], outs[2 * p + 1])
            o_ref[0, pl.ds(q0, tq), pair_slots[p]] = packed.astype(BF16)

    lead = LEFT_CHUNKS // BAND_GROUP
    for g in range(lead):
        group(g * tq, 0, (g + 1) * tq, LEFT_CHUNKS * CHUNK - g * tq)

    def body(i, _):
        for half in range(2):
            q0 = pl.multiple_of((lead + 2 * i + half) * tq, tq)
            group(q0, pl.multiple_of(q0 - LEFT_CHUNKS * CHUNK, tq), BAND_KEYS, 0)
        return 0

    n_groups = q_ref.shape[1] // tq - lead
    assert n_groups % 2 == 0
    lax.fori_loop(0, n_groups // 2, body, 0)


def _band_bias_body(rev_ref, o_ref):
    tq, n = ATT_TILE, rev_ref.shape[-1]
    rolled = pltpu.roll(jnp.broadcast_to(rev_ref[0], (tq, n)), 1, 1, stride=1, stride_axis=0)
    q_chunk = lax.broadcasted_iota(jnp.int32, (tq, BAND_KEYS), 0) // CHUNK
    k_chunk = lax.broadcasted_iota(jnp.int32, (tq, BAND_KEYS), 1) // CHUNK
    in_band = (k_chunk >= q_chunk) & (k_chunk <= q_chunk + LEFT_CHUNKS)
    o_ref[0] = jnp.where(in_band, rolled[:, tq:tq + BAND_KEYS] * LOG2E, -jnp.inf)


def _band_bias_table(rel_bias):
    tq, n_heads = ATT_TILE, rel_bias.shape[0]
    shift = LEFT_CHUNKS * CHUNK
    assert shift - (BAND_KEYS - 1) >= -REL_CLIP and shift + tq - 1 >= REL_CLIP
    n_rel = tq - 1 + BAND_KEYS
    n_clipped = n_rel - (tq - 1 + REL_CLIP + 1)
    by_rel = jnp.concatenate(
        [rel_bias[:, REL_CLIP - (tq - 1):], jnp.broadcast_to(rel_bias[:, -1:], (n_heads, n_clipped))], axis=1)
    rev = jnp.pad(by_rel[:, ::-1], ((0, 0), (0, 1)))
    assert rev.shape[1] == tq + BAND_KEYS and shift + tq - 1 == n_rel - tq
    return pl.pallas_call(
        _band_bias_body,
        grid=(n_heads,),
        in_specs=[pl.BlockSpec((1, 1, rev.shape[1]), lambda h: (h, 0, 0))],
        out_specs=pl.BlockSpec((1, tq, BAND_KEYS), lambda h: (h, 0, 0)),
        out_shape=jax.ShapeDtypeStruct((n_heads, tq, BAND_KEYS), F32),
        compiler_params=_params(1, 8 * tq * rev.shape[1] * 4),
        name="band_bias",
    )(rev.reshape(n_heads, 1, rev.shape[1]))


def _band_attention(q, k, v, bias_table):
    bsz, seq, _ = q.shape
    g = BAND_HEADS_PER_STEP
    tq = ATT_TILE
    width = g * C_DIM
    seq_spec = pl.BlockSpec((1, seq, width), lambda b, hg: (b, 0, hg))
    nbytes = 2 * (4 * seq * width * 2 + g * tq * BAND_KEYS * 4) + 4 * g * tq * BAND_KEYS * 4
    return pl.pallas_call(
        _band_body,
        grid=(bsz, C_HEADS // g),
        in_specs=[seq_spec, seq_spec, seq_spec,
                  pl.BlockSpec((g, tq, BAND_KEYS), lambda b, hg: (hg, 0, 0))],
        out_specs=seq_spec,
        out_shape=jax.ShapeDtypeStruct((bsz, seq, C_HEADS * C_DIM), BF16),
        compiler_params=_params(2, nbytes),
        name="band_attention",
    )(q, k, v, bias_table)


def _mix_ffn_body(*refs, n_mix, final_norm):
    h_ref, gffn_ref = refs[0], refs[1]
    mix_refs = refs[2:2 + n_mix]
    wo_ref, wg_ref, wu_ref, wd_ref, gfin_ref, o_ref, acc_ref = refs[2 + n_mix:]
    mixed = jnp.concatenate([r[...] for r in mix_refs], axis=1) if n_mix > 1 else mix_refs[0][...]
    h = h_ref[...] + _dot(mixed, wo_ref[...])
    u = _rms(h, gffn_ref[...]).astype(BF16)
    d_ff = wg_ref.shape[1]
    acc_ref[...] = h
    for f0 in range(0, d_ff, FFN_COL_TILE):
        cols = slice(f0, f0 + FFN_COL_TILE)
        gate = _dot(u, wg_ref[:, cols])
        act = (gate * jax.nn.sigmoid(gate) * _dot(u, wu_ref[:, cols])).astype(BF16)
        acc_ref[...] += _dot(act, wd_ref[cols, :])
    out = acc_ref[...]
    o_ref[...] = _rms(out, gfin_ref[...]) if final_norm else out


def _mix_ffn(h, g_ffn, mix_outs, w_out, wg, wu, wd, g_final, final_norm):
    t, d = h.shape
    tm = FFN_ROW_TILE
    d_ff = wg.shape[1]
    assert d_ff % FFN_COL_TILE == 0
    row = lambda n: pl.BlockSpec((tm, n), lambda i: (i, 0))
    nbytes = ((3 * d * d_ff + w_out.size) * 2 + 2 * tm * d * (4 + 4) + 2 * tm * w_out.shape[0] * 2
              + 3 * tm * d * 4 + 4 * tm * FFN_COL_TILE * 4)
    return pl.pallas_call(
        functools.partial(_mix_ffn_body, n_mix=len(mix_outs), final_norm=final_norm),
        grid=(t // tm,),
        in_specs=[row(d), _resident(g_ffn.shape)] + [row(o.shape[1]) for o in mix_outs]
                 + [_resident(w_out.shape), _resident(wg.shape), _resident(wu.shape), _resident(wd.shape),
                    _resident(g_final.shape)],
        out_specs=row(d),
        out_shape=jax.ShapeDtypeStruct((t, d), F32),
        scratch_shapes=[pltpu.VMEM((tm, d), F32)],
        compiler_params=_params(1, nbytes),
        name="mix_ffn",
    )(h, g_ffn, *mix_outs, w_out, wg, wu, wd, g_final)


def _cast_body(*refs):
    n = len(refs) // 2
    for src, dst in zip(refs[:n], refs[n:]):
        dst[...] = src[0].astype(BF16)


def _cast_weights(stacks, indices):
    steps = WEIGHT_CAST_STEPS
    blocks = [(w.shape[1] // steps, w.shape[2]) for w in stacks]
    assert all(w.shape[1] % (steps * 16) == 0 for w in stacks)
    nbytes = 2 * sum(r * c * 6 for r, c in blocks)
    return pl.pallas_call(
        _cast_body,
        grid=(steps,),
        in_specs=[pl.BlockSpec((1, r, c), lambda i, l=l: (l, i, 0)) for (r, c), l in zip(blocks, indices)],
        out_specs=[pl.BlockSpec((r, c), lambda i: (i, 0)) for r, c in blocks],
        out_shape=[jax.ShapeDtypeStruct(w.shape[1:], BF16) for w in stacks],
        compiler_params=_params(1, nbytes),
        name="cast_weights",
    )(*stacks)


def _head_slots(w, n_heads, per_head, start, keep):
    k = w.shape[0]
    w = w.reshape(k, n_heads, per_head)[:, :, start:start + keep]
    return jnp.pad(w, ((0, 0), (0, 0), (0, LANES - keep))).reshape(k, n_heads * LANES)


def _rope_slot_tables(seq, scale):
    half = MLA_ROPE // 2
    pos = jnp.arange(seq, dtype=F32)
    inv_freq = ROPE_THETA ** (-jnp.arange(0, MLA_ROPE, 2, dtype=F32) / MLA_ROPE)
    ang = pos[:, None] * inv_freq[None, :]
    cos, sin = jnp.cos(ang), jnp.sin(ang)
    zeros = jnp.zeros((seq, half), F32)
    pad = jnp.zeros((seq, LANES - MLA_NOPE - MLA_ROPE), F32)
    ones = jnp.ones((seq, MLA_NOPE), F32)
    direct = jnp.concatenate([ones, cos, cos, pad], axis=1)
    from_lower = jnp.concatenate([0 * ones, zeros, sin, pad], axis=1)
    from_upper = jnp.concatenate([0 * ones, -sin, zeros, pad], axis=1)
    return jnp.stack([direct, from_lower, from_upper]) * scale


def kernel(x, ev_w_in, ev_g_cq, ev_w_uq, ev_g_ckv, ev_w_ukv, ev_w_out, od_w_qkv, od_rel_bias, od_w_out,
           g_mix, g_ffn, w_gate, w_up, w_down, g_final):
    bsz, seq, d = x.shape
    depth = g_mix.shape[0]
    t = bsz * seq
    h = x.reshape(t, d)
    g_fin = g_final.reshape(1, d)
    rope_q = _rope_slot_tables(seq, (MLA_NOPE + MLA_ROPE) ** -0.5 * LOG2E)
    rope_k = _rope_slot_tables(seq, 1.0)

    for layer in range(depth):
        i = layer // 2
        gm = g_mix[layer].reshape(1, d)
        if layer % 2 == 0:
            w_in = ev_w_in[i]
            o1, o2, o3 = Q_LORA, Q_LORA + KV_LORA, Q_LORA + KV_LORA + MLA_ROPE
            wcq = w_in[:, :o1].astype(BF16)
            wckv = w_in[:, o1:o2].astype(BF16)
            wkr = jnp.pad(w_in[:, o2:o3], ((0, 0), (MLA_NOPE, LANES - MLA_NOPE - MLA_ROPE))).astype(BF16)
            wb = w_in[:, o3:].astype(BF16)
            wuq = _head_slots(ev_w_uq[i], MLA_HEADS, MLA_NOPE + MLA_ROPE, 0, MLA_NOPE + MLA_ROPE).astype(BF16)
            wuk = _head_slots(ev_w_ukv[i], MLA_HEADS, MLA_NOPE + MLA_V, 0, MLA_NOPE).astype(BF16)
            wuv = _head_slots(ev_w_ukv[i], MLA_HEADS, MLA_NOPE + MLA_V, MLA_NOPE, MLA_V).astype(BF16)
            qa, ka, va, qb, kb, vb = _even_proj(
                h, gm, wcq, wckv, wkr, wb, ev_g_cq[i].reshape(1, -1), wuq, ev_g_ckv[i].reshape(1, -1),
                wuk, wuv, rope_q, rope_k, seq)
            r3 = lambda a: a.reshape(bsz, seq, a.shape[1])
            o_a = _mla_attention(r3(qa), r3(ka), r3(va)).reshape(t, -1)
            o_b = _sb_attention(r3(qb), r3(kb), r3(vb)).reshape(t, -1)
            mix_outs, w_out_stack = [o_a, o_b], ev_w_out
        else:
            (w_qkv,) = _cast_weights([od_w_qkv], [i])
            q, k, v = _odd_proj(h, gm, w_qkv)
            r3 = lambda a: a.reshape(bsz, seq, a.shape[1])
            o_c = _band_attention(r3(q), r3(k), r3(v), _band_bias_table(od_rel_bias[i])).reshape(t, -1)
            mix_outs, w_out_stack = [o_c], od_w_out
        w_out, wg, wu, wd = _cast_weights([w_out_stack, w_gate, w_up, w_down], [i, layer, layer, layer])
        h = _mix_ffn(h, g_ffn[layer].reshape(1, d), mix_outs, w_out, wg, wu, wd, g_fin,
                     final_norm=(layer == depth - 1))
    return h.reshape(bsz, seq, d)
```

```python
import functools

import jax
import jax.numpy as jnp
from jax import lax
from jax.experimental import pallas as pl
from jax.experimental.pallas import tpu as pltpu

F32 = jnp.float32
BF16 = jnp.bfloat16

CHUNK = 64
MLA_HEADS = 8
MLA_NOPE = 64
MLA_ROPE = 32
MLA_V = 64
Q_LORA = 384
KV_LORA = 256
ROPE_THETA = 10000.0
SB_HEADS = 8
SB_DIM = 64
C_HEADS = 16
C_DIM = 64
LEFT_CHUNKS = 8
REL_CLIP = 256
RMS_EPS = 1e-6

LANES = 128
V7X_VMEM_BYTES = 64 * 1024 * 1024

ROW_TILE = 512
FFN_ROW_TILE = 512
FFN_COL_TILE = 256
Q_TILE = 512
K_TILE = 256
HEADS_PER_STEP = 4
MLA_K_TILE = 512
MLA_DIAG_TILE = 256
MLA_HEADS_PER_STEP = 4
BAND_HEADS_PER_STEP = 4
ATT_TILE = 256
BAND_GROUP = ATT_TILE // CHUNK
BAND_KEYS = (LEFT_CHUNKS + BAND_GROUP) * CHUNK
WEIGHT_CAST_STEPS = 8

NEG_BIG = -1e30
LOG2E = 1.4426950408889634
EXP_UNDERFLOW = -104.0


def _vmem_limit(nbytes):
    return int(min(max(2 * nbytes, 32 * 1024 * 1024), V7X_VMEM_BYTES - 8 * 1024 * 1024))


def _params(n_axes, nbytes):
    return pltpu.CompilerParams(
        dimension_semantics=("arbitrary",) * n_axes,
        vmem_limit_bytes=_vmem_limit(nbytes),
    )


def _dot(a, b):
    return jnp.dot(a, b, preferred_element_type=F32)


def _dot_nt(a, b):
    return lax.dot_general(a, b, (((1,), (1,)), ((), ())), preferred_element_type=F32)


def _rms(x, g):
    return x * lax.rsqrt(jnp.mean(x * x, axis=-1, keepdims=True) + RMS_EPS) * g


def _resident(shape):
    nd = len(shape)
    return pl.BlockSpec(shape, lambda *_: (0,) * nd, pipeline_mode=pl.Buffered(1))


def _rope(x, tab_ref):
    return (x * tab_ref[0]
            + pltpu.roll(x, MLA_ROPE // 2, 1) * tab_ref[1]
            + pltpu.roll(x, LANES - MLA_ROPE // 2, 1) * tab_ref[2])


def _even_proj_body(h_ref, gmix_ref, wcq_ref, wckv_ref, wkr_ref, wb_ref, gcq_ref, wuq_ref,
                    gckv_ref, wuk_ref, wuv_ref, ropeq_ref, ropek_ref,
                    qa_ref, ka_ref, va_ref, qb_ref, kb_ref, vb_ref):
    u = _rms(h_ref[...], gmix_ref[...]).astype(BF16)
    nb = SB_HEADS * SB_DIM
    b = _dot(u, wb_ref[...])
    qb_ref[...] = (b[:, :nb] * (SB_DIM ** -0.5)).astype(BF16)
    kb_ref[...] = b[:, nb:2 * nb].astype(BF16)
    vb_ref[...] = b[:, 2 * nb:].astype(BF16)

    cqn = _rms(_dot(u, wcq_ref[...]), gcq_ref[...]).astype(BF16)
    ckvn = _rms(_dot(u, wckv_ref[...]), gckv_ref[...]).astype(BF16)
    va = _dot(ckvn, wuv_ref[...])
    ones_lane = lax.broadcasted_iota(jnp.int32, va.shape, 1) % LANES == MLA_V
    va_ref[...] = jnp.where(ones_lane, 1.0, va).astype(BF16)

    k_rope = _rope(_dot(u, wkr_ref[...]), ropek_ref)
    qa = _dot(cqn, wuq_ref[...])
    kn = _dot(ckvn, wuk_ref[...])
    for hh in range(MLA_HEADS):
        sl = slice(hh * LANES, (hh + 1) * LANES)
        qa_ref[:, sl] = _rope(qa[:, sl], ropeq_ref).astype(BF16)
        ka_ref[:, sl] = (kn[:, sl] + k_rope).astype(BF16)


def _even_proj(h, gmix, wcq, wckv, wkr, wb, gcq, wuq, gckv, wuk, wuv, ropeq, ropek, seq):
    t, d = h.shape
    tm = ROW_TILE
    per_seq = seq // tm
    row = lambda n: pl.BlockSpec((tm, n), lambda i: (i, 0))
    rope_spec = pl.BlockSpec((3, tm, LANES), lambda i: (0, i % per_seq, 0))
    weights = (wcq, wckv, wkr, wb, gcq, wuq, gckv, wuk, wuv)
    out_widths = (MLA_HEADS * LANES,) * 3 + (SB_HEADS * SB_DIM,) * 3
    nbytes = (sum(w.size * w.dtype.itemsize for w in weights)
              + 2 * tm * d * 4 + 2 * tm * sum(out_widths) * 2 + tm * 8192 * 4)
    return pl.pallas_call(
        _even_proj_body,
        grid=(t // tm,),
        in_specs=[row(d), _resident(gmix.shape)]
                 + [_resident(wcq.shape), _resident(wckv.shape), _resident(wkr.shape), _resident(wb.shape),
                    _resident(gcq.shape), _resident(wuq.shape), _resident(gckv.shape), _resident(wuk.shape),
                    _resident(wuv.shape), rope_spec, rope_spec],
        out_specs=[row(n) for n in out_widths],
        out_shape=[jax.ShapeDtypeStruct((t, n), BF16) for n in out_widths],
        compiler_params=_params(1, nbytes),
        name="even_proj",
    )(h, gmix, wcq, wckv, wkr, wb, gcq, wuq, gckv, wuk, wuv, ropeq, ropek)


def _head_select(shape, head_in_pair):
    lane = lax.broadcasted_iota(jnp.int32, shape, 1)
    return (lane < SB_DIM) if head_in_pair == 0 else (lane >= SB_DIM)


def _mla_body(q_ref, k_ref, v_ref, o_ref):
    tq, tk, HEADS_PER_STEP = Q_TILE, MLA_K_TILE, MLA_HEADS_PER_STEP
    assert tq == tk
    dk = MLA_DIAG_TILE
    qi = pl.program_id(2)
    row_chunk = lax.broadcasted_iota(jnp.int32, (tq, dk), 0) // CHUNK
    col_chunk = lax.broadcasted_iota(jnp.int32, (tq, dk), 1) // CHUNK
    slots = [slice(hh * LANES, (hh + 1) * LANES) for hh in range(HEADS_PER_STEP)]
    qs = [q_ref[0, :, sl] for sl in slots]

    def tile(k0, n_keys, carry, visible, first_row=0):
        rows = pl.ds(pl.multiple_of(k0, n_keys), n_keys)
        heads = range(HEADS_PER_STEP)
        scores = [_dot_nt(qs[hh][first_row:], k_ref[0, rows, slots[hh]]) for hh in heads]
        new = []
        for hh in heads:
            m, acc = carry[hh]
            s = scores[hh] if visible is None else jnp.where(visible, scores[hh], -jnp.inf)
            m_new = jnp.maximum(m, jnp.max(s, axis=-1, keepdims=True))
            p = jnp.exp2(s - m_new).astype(BF16)
            new.append((m_new, jnp.exp2(m - m_new) * acc, p))
        return tuple((m_new, acc + _dot(p, v_ref[0, rows, slots[hh]]))
                     for hh, (m_new, acc, p) in zip(heads, new))

    init = ((jnp.full((tq, 1), NEG_BIG, F32), jnp.zeros((tq, LANES), F32)),) * HEADS_PER_STEP
    carry = lax.fori_loop(0, qi, lambda kb, c: tile(kb * tk, tk, c, None), init)
    for d in range(tq // dk):
        first = d * dk
        part = tuple((m[first:], acc[first:]) for m, acc in carry)
        part = tile(qi * tq + first, dk, part, (col_chunk <= row_chunk)[:tq - first], first_row=first)
        carry = tuple((jnp.concatenate([m[:first], pm], axis=0), jnp.concatenate([acc[:first], pacc], axis=0))
                      for (m, acc), (pm, pacc) in zip(carry, part)) if first else part
    outs = [acc / acc[:, MLA_V:MLA_V + 1] for _, acc in carry]
    lane = lax.broadcasted_iota(jnp.int32, (tq, LANES), 1)
    for p in range(HEADS_PER_STEP // 2):
        packed = jnp.where(lane < MLA_V, outs[2 * p], pltpu.roll(outs[2 * p + 1], MLA_V, 1))
        o_ref[0, :, p * LANES:(p + 1) * LANES] = packed.astype(BF16)


def _mla_attention(qa, ka, va):
    bsz, seq, _ = qa.shape
    tq, g = Q_TILE, MLA_HEADS_PER_STEP
    nbytes = 2 * (2 * seq * g * LANES * 2 + tq * g * LANES * 3) + 8 * g * tq * MLA_K_TILE * 4
    return pl.pallas_call(
        _mla_body,
        grid=(bsz, MLA_HEADS // g, seq // tq),
        in_specs=[pl.BlockSpec((1, tq, g * LANES), lambda b, hg, qi: (b, qi, hg)),
                  pl.BlockSpec((1, seq, g * LANES), lambda b, hg, qi: (b, 0, hg)),
                  pl.BlockSpec((1, seq, g * LANES), lambda b, hg, qi: (b, 0, hg))],
        out_specs=pl.BlockSpec((1, tq, g * MLA_V), lambda b, hg, qi: (b, qi, hg)),
        out_shape=jax.ShapeDtypeStruct((bsz, seq, MLA_HEADS * MLA_V), BF16),
        compiler_params=_params(3, nbytes),
        name="mla_attention",
    )(qa, ka, va)


def _sb_body(q_ref, k_ref, v_ref, ntri_ref, o_ref):
    tq, tk = Q_TILE, K_TILE
    qi = pl.program_id(2)
    per = tq // tk
    row = lax.broadcasted_iota(jnp.int32, (tq, tk), 0)
    col = lax.broadcasted_iota(jnp.int32, (tq, tk), 1)
    pair_slots = [slice(p * LANES, (p + 1) * LANES) for p in range(HEADS_PER_STEP // 2)]
    qs = []
    for hh in range(HEADS_PER_STEP):
        q_pair = q_ref[0, :, pair_slots[hh // 2]]
        qs.append(jnp.where(_head_select((tq, LANES), hh % 2), q_pair, jnp.zeros_like(q_pair)))

    def tile(kb, carry, before, first_row=0):
        rows = pl.ds(pl.multiple_of(kb * tk, tk), tk)
        heads = range(HEADS_PER_STEP)
        zs = [_dot_nt(qs[hh][first_row:], k_ref[0, rows, pair_slots[hh // 2]]) for hh in heads]
        softplus = []
        for hh in heads:
            sp = jnp.maximum(zs[hh], 0.0) + jnp.log(1.0 + jnp.exp2(jnp.abs(zs[hh]) * -LOG2E))
            softplus.append(sp if before is None else jnp.where(before, sp, 0.0))
        cums = [_dot(softplus[hh].astype(BF16), ntri_ref[...]) + carry[hh][0] for hh in heads]
        weights = []
        for hh in heads:
            w = jnp.exp((zs[hh] - softplus[hh]) + cums[hh])
            if before is not None:
                w = jnp.where(before, w, 0.0)
            weights.append(w.astype(BF16))
        return tuple((cums[hh][:, 0:1] - softplus[hh][:, 0:1],
                      carry[hh][1] + _dot(weights[hh], v_ref[0, rows, pair_slots[hh // 2]]))
                     for hh in heads)

    carry = ((jnp.zeros((tk, 1), F32), jnp.zeros((tk, LANES), F32)),) * HEADS_PER_STEP
    for d in reversed(range(per)):
        first = d * tk
        if d < per - 1:
            carry = tuple((jnp.concatenate([jnp.zeros((tk, 1), F32), later], axis=0),
                           jnp.concatenate([jnp.zeros((tk, LANES), F32), acc], axis=0)) for later, acc in carry)
        carry = tile(per * qi + d, carry, (col < row)[:tq - first], first_row=first)

    n_earlier = per * qi

    def any_weight_left(c):
        return (jnp.max(functools.reduce(jnp.maximum, [later for later, _ in c])) > EXP_UNDERFLOW).astype(jnp.int32)

    def earlier_tile(state):
        i, _, c = state
        c = tile(n_earlier - 1 - i, c, None)
        return i + 1, any_weight_left(c), c

    _, _, carry = lax.while_loop(lambda state: jnp.logical_and(state[0] < n_earlier, state[1] > 0),
                                 earlier_tile, (jnp.int32(0), any_weight_left(carry), carry))
    for p in range(HEADS_PER_STEP // 2):
        packed = jnp.where(_head_select((tq, LANES), 0), carry[2 * p][1], carry[2 * p + 1][1])
        o_ref[0, :, pair_slots[p]] = packed.astype(BF16)


def _sb_attention(qb, kb, vb):
    bsz, seq, _ = qb.shape
    tq, tk, g = Q_TILE, K_TILE, HEADS_PER_STEP
    width = g * SB_DIM
    j = jnp.arange(tk)
    ntri2 = -(j[:, None] > j[None, :]).astype(BF16)
    nbytes = 2 * (2 * seq * width * 2 + 2 * tq * width * 2) + tk * tk * 2 + 8 * g * tq * tk * 4
    return pl.pallas_call(
        _sb_body,
        grid=(bsz, SB_HEADS // g, seq // tq),
        in_specs=[pl.BlockSpec((1, tq, width), lambda b, hg, qi: (b, qi, hg)),
                  pl.BlockSpec((1, seq, width), lambda b, hg, qi: (b, 0, hg)),
                  pl.BlockSpec((1, seq, width), lambda b, hg, qi: (b, 0, hg)),
                  _resident(ntri2.shape)],
        out_specs=pl.BlockSpec((1, tq, width), lambda b, hg, qi: (b, qi, hg)),
        out_shape=jax.ShapeDtypeStruct((bsz, seq, SB_HEADS * SB_DIM), BF16),
        compiler_params=_params(3, nbytes),
        name="sb_attention",
    )(qb, kb, vb, ntri2)


def _odd_proj_body(h_ref, gmix_ref, w_ref, q_ref, k_ref, v_ref):
    u = _rms(h_ref[...], gmix_ref[...]).astype(BF16)
    n = C_HEADS * C_DIM
    qkv = _dot(u, w_ref[...])
    q_ref[...] = (qkv[:, :n] * (C_DIM ** -0.5 * LOG2E)).astype(BF16)
    k_ref[...] = qkv[:, n:2 * n].astype(BF16)
    v_ref[...] = qkv[:, 2 * n:].astype(BF16)


def _odd_proj(h, gmix, w_qkv):
    t, d = h.shape
    tm = ROW_TILE
    n = C_HEADS * C_DIM
    row = lambda m: pl.BlockSpec((tm, m), lambda i: (i, 0))
    nbytes = w_qkv.size * 2 + 2 * tm * d * 4 + 2 * 3 * tm * n * 2 + 2 * tm * 3 * n * 4
    return pl.pallas_call(
        _odd_proj_body,
        grid=(t // tm,),
        in_specs=[row(d), _resident(gmix.shape), _resident(w_qkv.shape)],
        out_specs=[row(n)] * 3,
        out_shape=[jax.ShapeDtypeStruct((t, n), BF16)] * 3,
        compiler_params=_params(1, nbytes),
        name="odd_proj",
    )(h, gmix, w_qkv)


def _band_body(q_ref, k_ref, v_ref, bias_ref, o_ref):
    tq = ATT_TILE

    def group(q0, k0, n_keys, bias_off):
        heads = range(BAND_HEADS_PER_STEP)
        pair_slots = [slice(p * LANES, (p + 1) * LANES) for p in range(BAND_HEADS_PER_STEP // 2)]
        scores = []
        for hh in heads:
            q_pair = q_ref[0, pl.ds(q0, tq), pair_slots[hh // 2]]
            q = jnp.where(_head_select((tq, LANES), hh % 2), q_pair, jnp.zeros_like(q_pair))
            scores.append(_dot_nt(q, k_ref[0, pl.ds(k0, n_keys), pair_slots[hh // 2]]))
        probs, denoms = [], []
        for hh in heads:
            s = scores[hh] + bias_ref[hh, :, bias_off:bias_off + n_keys]
            p = jnp.exp2(s - jnp.max(s, axis=-1, keepdims=True))
            denoms.append(jnp.sum(p, axis=-1, keepdims=True))
            probs.append(p.astype(BF16))
        outs = [_dot(probs[hh], v_ref[0, pl.ds(k0, n_keys), pair_slots[hh // 2]]) / denoms[hh] for hh in heads]
        for p in range(BAND_HEADS_PER_STEP // 2):
            packed = jnp.where(_head_select((tq, LANES), 0), outs[2 * p], outs[2 * p + 1])
            o_ref[0, pl.ds(q0, tq), pair_slots[p]] = packed.astype(BF16)

    lead = LEFT_CHUNKS // BAND_GROUP
    for g in range(lead):
        group(g * tq, 0, (g + 1) * tq, LEFT_CHUNKS * CHUNK - g * tq)

    def body(i, _):
        for half in range(2):
            q0 = pl.multiple_of((lead + 2 * i + half) * tq, tq)
            group(q0, pl.multiple_of(q0 - LEFT_CHUNKS * CHUNK, tq), BAND_KEYS, 0)
        return 0

    n_groups = q_ref.shape[1] // tq - lead
    assert n_groups % 2 == 0
    lax.fori_loop(0, n_groups // 2, body, 0)


def _band_bias_body(rev_ref, o_ref):
    tq, n = ATT_TILE, rev_ref.shape[-1]
    rolled = pltpu.roll(jnp.broadcast_to(rev_ref[0], (tq, n)), 1, 1, stride=1, stride_axis=0)
    q_chunk = lax.broadcasted_iota(jnp.int32, (tq, BAND_KEYS), 0) // CHUNK
    k_chunk = lax.broadcasted_iota(jnp.int32, (tq, BAND_KEYS), 1) // CHUNK
    in_band = (k_chunk >= q_chunk) & (k_chunk <= q_chunk + LEFT_CHUNKS)
    o_ref[0] = jnp.where(in_band, rolled[:, tq:tq + BAND_KEYS] * LOG2E, -jnp.inf)


def _band_bias_table(rel_bias):
    tq, n_heads = ATT_TILE, rel_bias.shape[0]
    shift = LEFT_CHUNKS * CHUNK
    assert shift - (BAND_KEYS - 1) >= -REL_CLIP and shift + tq - 1 >= REL_CLIP
    n_rel = tq - 1 + BAND_KEYS
    n_clipped = n_rel - (tq - 1 + REL_CLIP + 1)
    by_rel = jnp.concatenate(
        [rel_bias[:, REL_CLIP - (tq - 1):], jnp.broadcast_to(rel_bias[:, -1:], (n_heads, n_clipped))], axis=1)
    rev = jnp.pad(by_rel[:, ::-1], ((0, 0), (0, 1)))
    assert rev.shape[1] == tq + BAND_KEYS and shift + tq - 1 == n_rel - tq
    return pl.pallas_call(
        _band_bias_body,
        grid=(n_heads,),
        in_specs=[pl.BlockSpec((1, 1, rev.shape[1]), lambda h: (h, 0, 0))],
        out_specs=pl.BlockSpec((1, tq, BAND_KEYS), lambda h: (h, 0, 0)),
        out_shape=jax.ShapeDtypeStruct((n_heads, tq, BAND_KEYS), F32),
        compiler_params=_params(1, 8 * tq * rev.shape[1] * 4),
        name="band_bias",
    )(rev.reshape(n_heads, 1, rev.shape[1]))


def _band_attention(q, k, v, bias_table):
    bsz, seq, _ = q.shape
    g = BAND_HEADS_PER_STEP
    tq = ATT_TILE
    width = g * C_DIM
    seq_spec = pl.BlockSpec((1, seq, width), lambda b, hg: (b, 0, hg))
    nbytes = 2 * (4 * seq * width * 2 + g * tq * BAND_KEYS * 4) + 4 * g * tq * BAND_KEYS * 4
    return pl.pallas_call(
        _band_body,
        grid=(bsz, C_HEADS // g),
        in_specs=[seq_spec, seq_spec, seq_spec,
                  pl.BlockSpec((g, tq, BAND_KEYS), lambda b, hg: (hg, 0, 0))],
        out_specs=seq_spec,
        out_shape=jax.ShapeDtypeStruct((bsz, seq, C_HEADS * C_DIM), BF16),
        compiler_params=_params(2, nbytes),
        name="band_attention",
    )(q, k, v, bias_table)


def _mix_ffn_body(*refs, n_mix, final_norm):
    h_ref, gffn_ref = refs[0], refs[1]
    mix_refs = refs[2:2 + n_mix]
    wo_ref, wg_ref, wu_ref, wd_ref, gfin_ref, o_ref, acc_ref = refs[2 + n_mix:]
    mixed = jnp.concatenate([r[...] for r in mix_refs], axis=1) if n_mix > 1 else mix_refs[0][...]
    h = h_ref[...] + _dot(mixed, wo_ref[...])
    u = _rms(h, gffn_ref[...]).astype(BF16)
    d_ff = wg_ref.shape[1]
    acc_ref[...] = h
    for f0 in range(0, d_ff, FFN_COL_TILE):
        cols = slice(f0, f0 + FFN_COL_TILE)
        gate = _dot(u, wg_ref[:, cols])
        act = (gate * jax.nn.sigmoid(gate) * _dot(u, wu_ref[:, cols])).astype(BF16)
        acc_ref[...] += _dot(act, wd_ref[cols, :])
    out = acc_ref[...]
    o_ref[...] = _rms(out, gfin_ref[...]) if final_norm else out


def _mix_ffn(h, g_ffn, mix_outs, w_out, wg, wu, wd, g_final, final_norm):
    t, d = h.shape
    tm = FFN_ROW_TILE
    d_ff = wg.shape[1]
    assert d_ff % FFN_COL_TILE == 0
    row = lambda n: pl.BlockSpec((tm, n), lambda i: (i, 0))
    nbytes = ((3 * d * d_ff + w_out.size) * 2 + 2 * tm * d * (4 + 4) + 2 * tm * w_out.shape[0] * 2
              + 3 * tm * d * 4 + 4 * tm * FFN_COL_TILE * 4)
    return pl.pallas_call(
        functools.partial(_mix_ffn_body, n_mix=len(mix_outs), final_norm=final_norm),
        grid=(t // tm,),
        in_specs=[row(d), _resident(g_ffn.shape)] + [row(o.shape[1]) for o in mix_outs]
                 + [_resident(w_out.shape), _resident(wg.shape), _resident(wu.shape), _resident(wd.shape),
                    _resident(g_final.shape)],
        out_specs=row(d),
        out_shape=jax.ShapeDtypeStruct((t, d), F32),
        scratch_shapes=[pltpu.VMEM((tm, d), F32)],
        compiler_params=_params(1, nbytes),
        name="mix_ffn",
    )(h, g_ffn, *mix_outs, w_out, wg, wu, wd, g_final)


def _cast_body(*refs):
    n = len(refs) // 2
    for src, dst in zip(refs[:n], refs[n:]):
        dst[...] = src[0].astype(BF16)


def _cast_weights(stacks, indices):
    steps = WEIGHT_CAST_STEPS
    blocks = [(w.shape[1] // steps, w.shape[2]) for w in stacks]
    assert all(w.shape[1] % (steps * 16) == 0 for w in stacks)
    nbytes = 2 * sum(r * c * 6 for r, c in blocks)
    return pl.pallas_call(
        _cast_body,
        grid=(steps,),
        in_specs=[pl.BlockSpec((1, r, c), lambda i, l=l: (l, i, 0)) for (r, c), l in zip(blocks, indices)],
        out_specs=[pl.BlockSpec((r, c), lambda i: (i, 0)) for r, c in blocks],
        out_shape=[jax.ShapeDtypeStruct(w.shape[1:], BF16) for w in stacks],
        compiler_params=_params(1, nbytes),
        name="cast_weights",
    )(*stacks)


def _head_slots(w, n_heads, per_head, start, keep):
    k = w.shape[0]
    w = w.reshape(k, n_heads, per_head)[:, :, start:start + keep]
    return jnp.pad(w, ((0, 0), (0, 0), (0, LANES - keep))).reshape(k, n_heads * LANES)


def _rope_slot_tables(seq, scale):
    half = MLA_ROPE // 2
    pos = jnp.arange(seq, dtype=F32)
    inv_freq = ROPE_THETA ** (-jnp.arange(0, MLA_ROPE, 2, dtype=F32) / MLA_ROPE)
    ang = pos[:, None] * inv_freq[None, :]
    cos, sin = jnp.cos(ang), jnp.sin(ang)
    zeros = jnp.zeros((seq, half), F32)
    pad = jnp.zeros((seq, LANES - MLA_NOPE - MLA_ROPE), F32)
    ones = jnp.ones((seq, MLA_NOPE), F32)
    direct = jnp.concatenate([ones, cos, cos, pad], axis=1)
    from_lower = jnp.concatenate([0 * ones, zeros, sin, pad], axis=1)
    from_upper = jnp.concatenate([0 * ones, -sin, zeros, pad], axis=1)
    return jnp.stack([direct, from_lower, from_upper]) * scale


def kernel(x, ev_w_in, ev_g_cq, ev_w_uq, ev_g_ckv, ev_w_ukv, ev_w_out, od_w_qkv, od_rel_bias, od_w_out,
           g_mix, g_ffn, w_gate, w_up, w_down, g_final):
    bsz, seq, d = x.shape
    depth = g_mix.shape[0]
    t = bsz * seq
    h = x.reshape(t, d)
    g_fin = g_final.reshape(1, d)
    rope_q = _rope_slot_tables(seq, (MLA_NOPE + MLA_ROPE) ** -0.5 * LOG2E)
    rope_k = _rope_slot_tables(seq, 1.0)

    for layer in range(depth):
        i = layer // 2
        gm = g_mix[layer].reshape(1, d)
        if layer % 2 == 0:
            w_in = ev_w_in[i]
            o1, o2, o3 = Q_LORA, Q_LORA + KV_LORA, Q_LORA + KV_LORA + MLA_ROPE
            wcq = w_in[:, :o1].astype(BF16)
            wckv = w_in[:, o1:o2].astype(BF16)
            wkr = jnp.pad(w_in[:, o2:o3], ((0, 0), (MLA_NOPE, LANES - MLA_NOPE - MLA_ROPE))).astype(BF16)
            wb = w_in[:, o3:].astype(BF16)
            wuq = _head_slots(ev_w_uq[i], MLA_HEADS, MLA_NOPE + MLA_ROPE, 0, MLA_NOPE + MLA_ROPE).astype(BF16)
            wuk = _head_slots(ev_w_ukv[i], MLA_HEADS, MLA_NOPE + MLA_V, 0, MLA_NOPE).astype(BF16)
            wuv = _head_slots(ev_w_ukv[i], MLA_HEADS, MLA_NOPE + MLA_V, MLA_NOPE, MLA_V).astype(BF16)
            qa, ka, va, qb, kb, vb = _even_proj(
                h, gm, wcq, wckv, wkr, wb, ev_g_cq[i].reshape(1, -1), wuq, ev_g_ckv[i].reshape(1, -1),
                wuk, wuv, rope_q, rope_k, seq)
            r3 = lambda a: a.reshape(bsz, seq, a.shape[1])
            o_a = _mla_attention(r3(qa), r3(ka), r3(va)).reshape(t, -1)
            o_b = _sb_attention(r3(qb), r3(kb), r3(vb)).reshape(t, -1)
            mix_outs, w_out_stack = [o_a, o_b], ev_w_out
        else:
            (w_qkv,) = _cast_weights([od_w_qkv], [i])
            q, k, v = _odd_proj(h, gm, w_qkv)
            r3 = lambda a: a.reshape(bsz, seq, a.shape[1])
            o_c = _band_attention(r3(q), r3(k), r3(v), _band_bias_table(od_rel_bias[i])).reshape(t, -1)
            mix_outs, w_out_stack = [o_c], od_w_out
        w_out, wg, wu, wd = _cast_weights([w_out_stack, w_gate, w_up, w_down], [i, layer, layer, layer])
        h = _mix_ffn(h, g_ffn[layer].reshape(1, d), mix_outs, w_out, wg, wu, wd, g_fin,
                     final_norm=(layer == depth - 1))
    return h.reshape(bsz, seq, d)
```

```python
import functools

import jax
import jax.numpy as jnp
from jax import lax
from jax.experimental import pallas as pl
from jax.experimental.pallas import tpu as pltpu

F32 = jnp.float32
BF16 = jnp.bfloat16

CHUNK = 64
MLA_HEADS = 8
MLA_NOPE = 64
MLA_ROPE = 32
MLA_V = 64
Q_LORA = 384
KV_LORA = 256
ROPE_THETA = 10000.0
SB_HEADS = 8
SB_DIM = 64
C_HEADS = 16
C_DIM = 64
LEFT_CHUNKS = 8
REL_CLIP = 256
RMS_EPS = 1e-6

LANES = 128
V7X_VMEM_BYTES = 64 * 1024 * 1024

ROW_TILE = 512
FFN_ROW_TILE = 512
FFN_COL_TILE = 256
Q_TILE = 512
K_TILE = 256
HEADS_PER_STEP = 4
MLA_K_TILE = 512
MLA_DIAG_TILE = 256
MLA_HEADS_PER_STEP = 4
BAND_HEADS_PER_STEP = 8
ATT_TILE = 256
BAND_GROUP = ATT_TILE // CHUNK
BAND_KEYS = (LEFT_CHUNKS + BAND_GROUP) * CHUNK
WEIGHT_CAST_STEPS = 8

NEG_BIG = -1e30
LOG2E = 1.4426950408889634
EXP_UNDERFLOW = -104.0


def _vmem_limit(nbytes):
    return int(min(max(2 * nbytes, 32 * 1024 * 1024), V7X_VMEM_BYTES - 8 * 1024 * 1024))


def _params(n_axes, nbytes):
    return pltpu.CompilerParams(
        dimension_semantics=("arbitrary",) * n_axes,
        vmem_limit_bytes=_vmem_limit(nbytes),
    )


def _dot(a, b):
    return jnp.dot(a, b, preferred_element_type=F32)


def _dot_nt(a, b):
    return lax.dot_general(a, b, (((1,), (1,)), ((), ())), preferred_element_type=F32)


def _rms(x, g):
    return x * lax.rsqrt(jnp.mean(x * x, axis=-1, keepdims=True) + RMS_EPS) * g


def _resident(shape):
    nd = len(shape)
    return pl.BlockSpec(shape, lambda *_: (0,) * nd, pipeline_mode=pl.Buffered(1))


def _rope(x, tab_ref):
    return (x * tab_ref[0]
            + pltpu.roll(x, MLA_ROPE // 2, 1) * tab_ref[1]
            + pltpu.roll(x, LANES - MLA_ROPE // 2, 1) * tab_ref[2])


def _even_proj_body(h_ref, gmix_ref, wcq_ref, wckv_ref, wkr_ref, wb_ref, gcq_ref, wuq_ref,
                    gckv_ref, wuk_ref, wuv_ref, ropeq_ref, ropek_ref,
                    qa_ref, ka_ref, va_ref, qb_ref, kb_ref, vb_ref):
    u = _rms(h_ref[...], gmix_ref[...]).astype(BF16)
    nb = SB_HEADS * SB_DIM
    cq = _dot(u, wcq_ref[...])
    ckv = _dot(u, wckv_ref[...])
    qkb = _dot(u, wb_ref[:, :2 * nb])
    qb_ref[...] = (qkb[:, :nb] * (SB_DIM ** -0.5)).astype(BF16)
    kb_ref[...] = qkb[:, nb:].astype(BF16)
    cqn = _rms(cq, gcq_ref[...]).astype(BF16)
    ckvn = _rms(ckv, gckv_ref[...]).astype(BF16)
    va = _dot(ckvn, wuv_ref[...])
    ones_lane = lax.broadcasted_iota(jnp.int32, va.shape, 1) % LANES == MLA_V
    va_ref[...] = jnp.where(ones_lane, 1.0, va).astype(BF16)

    k_rope = _rope(_dot(u, wkr_ref[...]), ropek_ref)
    qa = _dot(cqn, wuq_ref[...])
    kn = _dot(ckvn, wuk_ref[...])
    for hh in range(MLA_HEADS):
        sl = slice(hh * LANES, (hh + 1) * LANES)
        qa_ref[:, sl] = _rope(qa[:, sl], ropeq_ref).astype(BF16)
        ka_ref[:, sl] = (kn[:, sl] + k_rope).astype(BF16)

    vb_ref[...] = _dot(u, wb_ref[:, 2 * nb:]).astype(BF16)


def _even_proj(h, gmix, wcq, wckv, wkr, wb, gcq, wuq, gckv, wuk, wuv, ropeq, ropek, seq):
    t, d = h.shape
    tm = ROW_TILE
    per_seq = seq // tm
    row = lambda n: pl.BlockSpec((tm, n), lambda i: (i, 0))
    rope_spec = pl.BlockSpec((3, tm, LANES), lambda i: (0, i % per_seq, 0))
    weights = (wcq, wckv, wkr, wb, gcq, wuq, gckv, wuk, wuv)
    out_widths = (MLA_HEADS * LANES,) * 3 + (SB_HEADS * SB_DIM,) * 3
    nbytes = (sum(w.size * w.dtype.itemsize for w in weights)
              + 2 * tm * d * 4 + 2 * tm * sum(out_widths) * 2 + tm * 8192 * 4)
    return pl.pallas_call(
        _even_proj_body,
        grid=(t // tm,),
        in_specs=[row(d), _resident(gmix.shape)]
                 + [_resident(wcq.shape), _resident(wckv.shape), _resident(wkr.shape), _resident(wb.shape),
                    _resident(gcq.shape), _resident(wuq.shape), _resident(gckv.shape), _resident(wuk.shape),
                    _resident(wuv.shape), rope_spec, rope_spec],
        out_specs=[row(n) for n in out_widths],
        out_shape=[jax.ShapeDtypeStruct((t, n), BF16) for n in out_widths],
        compiler_params=_params(1, nbytes),
        name="even_proj",
    )(h, gmix, wcq, wckv, wkr, wb, gcq, wuq, gckv, wuk, wuv, ropeq, ropek)


def _head_select(shape, head_in_pair):
    lane = lax.broadcasted_iota(jnp.int32, shape, 1)
    return (lane < SB_DIM) if head_in_pair == 0 else (lane >= SB_DIM)


def _mla_body(q_ref, k_ref, v_ref, o_ref):
    tq, tk, HEADS_PER_STEP = Q_TILE, MLA_K_TILE, MLA_HEADS_PER_STEP
    assert tq == tk
    dk = MLA_DIAG_TILE
    qi = pl.program_id(2)
    row_chunk = lax.broadcasted_iota(jnp.int32, (tq, dk), 0) // CHUNK
    col_chunk = lax.broadcasted_iota(jnp.int32, (tq, dk), 1) // CHUNK
    slots = [slice(hh * LANES, (hh + 1) * LANES) for hh in range(HEADS_PER_STEP)]
    qs = [q_ref[0, :, sl] for sl in slots]

    def tile(k0, n_keys, carry, visible, first_row=0):
        rows = pl.ds(pl.multiple_of(k0, n_keys), n_keys)
        heads = range(HEADS_PER_STEP)
        scores = [_dot_nt(qs[hh][first_row:], k_ref[0, rows, slots[hh]]) for hh in heads]
        new = []
        for hh in heads:
            m, acc = carry[hh]
            s = scores[hh] if visible is None else jnp.where(visible, scores[hh], -jnp.inf)
            m_new = jnp.maximum(m, jnp.max(s, axis=-1, keepdims=True))
            p = jnp.exp2(s - m_new).astype(BF16)
            new.append((m_new, jnp.exp2(m - m_new) * acc, p))
        return tuple((m_new, acc + _dot(p, v_ref[0, rows, slots[hh]]))
                     for hh, (m_new, acc, p) in zip(heads, new))

    init = ((jnp.full((tq, 1), NEG_BIG, F32), jnp.zeros((tq, LANES), F32)),) * HEADS_PER_STEP
    carry = lax.fori_loop(0, qi, lambda kb, c: tile(kb * tk, tk, c, None), init)
    for d in range(tq // dk):
        first = d * dk
        part = tuple((m[first:], acc[first:]) for m, acc in carry)
        part = tile(qi * tq + first, dk, part, (col_chunk <= row_chunk)[:tq - first], first_row=first)
        carry = tuple((jnp.concatenate([m[:first], pm], axis=0), jnp.concatenate([acc[:first], pacc], axis=0))
                      for (m, acc), (pm, pacc) in zip(carry, part)) if first else part
    outs = [acc / acc[:, MLA_V:MLA_V + 1] for _, acc in carry]
    lane = lax.broadcasted_iota(jnp.int32, (tq, LANES), 1)
    for p in range(HEADS_PER_STEP // 2):
        packed = jnp.where(lane < MLA_V, outs[2 * p], pltpu.roll(outs[2 * p + 1], MLA_V, 1))
        o_ref[0, :, p * LANES:(p + 1) * LANES] = packed.astype(BF16)


def _mla_attention(qa, ka, va):
    bsz, seq, _ = qa.shape
    tq, g = Q_TILE, MLA_HEADS_PER_STEP
    nbytes = 2 * (2 * seq * g * LANES * 2 + tq * g * LANES * 3) + 8 * g * tq * MLA_K_TILE * 4
    return pl.pallas_call(
        _mla_body,
        grid=(bsz, MLA_HEADS // g, seq // tq),
        in_specs=[pl.BlockSpec((1, tq, g * LANES), lambda b, hg, qi: (b, qi, hg)),
                  pl.BlockSpec((1, seq, g * LANES), lambda b, hg, qi: (b, 0, hg)),
                  pl.BlockSpec((1, seq, g * LANES), lambda b, hg, qi: (b, 0, hg))],
        out_specs=pl.BlockSpec((1, tq, g * MLA_V), lambda b, hg, qi: (b, qi, hg)),
        out_shape=jax.ShapeDtypeStruct((bsz, seq, MLA_HEADS * MLA_V), BF16),
        compiler_params=_params(3, nbytes),
        name="mla_attention",
    )(qa, ka, va)


def _sb_body(q_ref, k_ref, v_ref, ntri_ref, o_ref):
    tq, tk = Q_TILE, K_TILE
    qi = pl.program_id(2)
    per = tq // tk
    row = lax.broadcasted_iota(jnp.int32, (tq, tk), 0)
    col = lax.broadcasted_iota(jnp.int32, (tq, tk), 1)
    pair_slots = [slice(p * LANES, (p + 1) * LANES) for p in range(HEADS_PER_STEP // 2)]
    qs = []
    for hh in range(HEADS_PER_STEP):
        q_pair = q_ref[0, :, pair_slots[hh // 2]]
        qs.append(jnp.where(_head_select((tq, LANES), hh % 2), q_pair, jnp.zeros_like(q_pair)))

    def tile(kb, carry, before, first_row=0):
        rows = pl.ds(pl.multiple_of(kb * tk, tk), tk)
        heads = range(HEADS_PER_STEP)
        zs = [_dot_nt(qs[hh][first_row:], k_ref[0, rows, pair_slots[hh // 2]]) for hh in heads]
        softplus = []
        for hh in heads:
            sp = jnp.maximum(zs[hh], 0.0) + jnp.log(1.0 + jnp.exp2(jnp.abs(zs[hh]) * -LOG2E))
            softplus.append(sp if before is None else jnp.where(before, sp, 0.0))
        cums = [_dot(softplus[hh].astype(BF16), ntri_ref[...]) + carry[hh][0] for hh in heads]
        weights = []
        for hh in heads:
            w = jnp.exp((zs[hh] - softplus[hh]) + cums[hh])
            if before is not None:
                w = jnp.where(before, w, 0.0)
            weights.append(w.astype(BF16))
        return tuple((cums[hh][:, 0:1] - softplus[hh][:, 0:1],
                      carry[hh][1] + _dot(weights[hh], v_ref[0, rows, pair_slots[hh // 2]]))
                     for hh in heads)

    carry = ((jnp.zeros((tk, 1), F32), jnp.zeros((tk, LANES), F32)),) * HEADS_PER_STEP
    for d in reversed(range(per)):
        first = d * tk
        if d < per - 1:
            carry = tuple((jnp.concatenate([jnp.zeros((tk, 1), F32), later], axis=0),
                           jnp.concatenate([jnp.zeros((tk, LANES), F32), acc], axis=0)) for later, acc in carry)
        carry = tile(per * qi + d, carry, (col < row)[:tq - first], first_row=first)

    n_earlier = per * qi

    def any_weight_left(c):
        return (jnp.max(functools.reduce(jnp.maximum, [later for later, _ in c])) > EXP_UNDERFLOW).astype(jnp.int32)

    def earlier_tile(state):
        i, _, c = state
        c = tile(n_earlier - 1 - i, c, None)
        return i + 1, any_weight_left(c), c

    _, _, carry = lax.while_loop(lambda state: jnp.logical_and(state[0] < n_earlier, state[1] > 0),
                                 earlier_tile, (jnp.int32(0), any_weight_left(carry), carry))
    for p in range(HEADS_PER_STEP // 2):
        packed = jnp.where(_head_select((tq, LANES), 0), carry[2 * p][1], carry[2 * p + 1][1])
        o_ref[0, :, pair_slots[p]] = packed.astype(BF16)


def _sb_attention(qb, kb, vb):
    bsz, seq, _ = qb.shape
    tq, tk, g = Q_TILE, K_TILE, HEADS_PER_STEP
    width = g * SB_DIM
    j = jnp.arange(tk)
    ntri2 = -(j[:, None] > j[None, :]).astype(BF16)
    nbytes = 2 * (2 * seq * width * 2 + 2 * tq * width * 2) + tk * tk * 2 + 8 * g * tq * tk * 4
    return pl.pallas_call(
        _sb_body,
        grid=(bsz, SB_HEADS // g, seq // tq),
        in_specs=[pl.BlockSpec((1, tq, width), lambda b, hg, qi: (b, qi, hg)),
                  pl.BlockSpec((1, seq, width), lambda b, hg, qi: (b, 0, hg)),
                  pl.BlockSpec((1, seq, width), lambda b, hg, qi: (b, 0, hg)),
                  _resident(ntri2.shape)],
        out_specs=pl.BlockSpec((1, tq, width), lambda b, hg, qi: (b, qi, hg)),
        out_shape=jax.ShapeDtypeStruct((bsz, seq, SB_HEADS * SB_DIM), BF16),
        compiler_params=_params(3, nbytes),
        name="sb_attention",
    )(qb, kb, vb, ntri2)


def _odd_proj_body(h_ref, gmix_ref, w_ref, q_ref, k_ref, v_ref):
    u = _rms(h_ref[...], gmix_ref[...]).astype(BF16)
    n = C_HEADS * C_DIM
    qkv = _dot(u, w_ref[...])
    q_ref[...] = (qkv[:, :n] * (C_DIM ** -0.5 * LOG2E)).astype(BF16)
    k_ref[...] = qkv[:, n:2 * n].astype(BF16)
    v_ref[...] = qkv[:, 2 * n:].astype(BF16)


def _odd_proj(h, gmix, w_qkv):
    t, d = h.shape
    tm = ROW_TILE
    n = C_HEADS * C_DIM
    row = lambda m: pl.BlockSpec((tm, m), lambda i: (i, 0))
    nbytes = w_qkv.size * 2 + 2 * tm * d * 4 + 2 * 3 * tm * n * 2 + 2 * tm * 3 * n * 4
    return pl.pallas_call(
        _odd_proj_body,
        grid=(t // tm,),
        in_specs=[row(d), _resident(gmix.shape), _resident(w_qkv.shape)],
        out_specs=[row(n)] * 3,
        out_shape=[jax.ShapeDtypeStruct((t, n), BF16)] * 3,
        compiler_params=_params(1, nbytes),
        name="odd_proj",
    )(h, gmix, w_qkv)


def _band_body(q_ref, k_ref, v_ref, bias_ref, o_ref):
    tq = ATT_TILE

    def group(q0, k0, n_keys, bias_off):
        heads = range(BAND_HEADS_PER_STEP)
        pair_slots = [slice(p * LANES, (p + 1) * LANES) for p in range(BAND_HEADS_PER_STEP // 2)]
        scores = []
        for hh in heads:
            q_pair = q_ref[0, pl.ds(q0, tq), pair_slots[hh // 2]]
            q = jnp.where(_head_select((tq, LANES), hh % 2), q_pair, jnp.zeros_like(q_pair))
            scores.append(_dot_nt(q, k_ref[0, pl.ds(k0, n_keys), pair_slots[hh // 2]]))
        probs = []
        for hh in heads:
            s = scores[hh] + bias_ref[hh, :, bias_off:bias_off + n_keys]
            probs.append(jnp.exp2(s - jnp.max(s, axis=-1, keepdims=True)).astype(BF16))
        outs = []
        for hh in heads:
            v_pair = v_ref[0, pl.ds(k0, n_keys), pair_slots[hh // 2]]
            vals = jnp.where(_head_select((n_keys, LANES), hh % 2), v_pair, jnp.ones_like(v_pair))
            pv = _dot(probs[hh], vals)
            row_sum_lane = (1 - hh % 2) * C_DIM
            outs.append(pv / pv[:, row_sum_lane:row_sum_lane + 1])
        for p in range(BAND_HEADS_PER_STEP // 2):
            packed = jnp.where(_head_select((tq, LANES), 0), outs[2 * p], outs[2 * p + 1])
            o_ref[0, pl.ds(q0, tq), pair_slots[p]] = packed.astype(BF16)

    lead = LEFT_CHUNKS // BAND_GROUP
    for g in range(lead):
        group(g * tq, 0, (g + 1) * tq, LEFT_CHUNKS * CHUNK - g * tq)

    def body(i, _):
        for half in range(2):
            q0 = pl.multiple_of((lead + 2 * i + half) * tq, tq)
            group(q0, pl.multiple_of(q0 - LEFT_CHUNKS * CHUNK, tq), BAND_KEYS, 0)
        return 0

    n_groups = q_ref.shape[1] // tq - lead
    assert n_groups % 2 == 0
    lax.fori_loop(0, n_groups // 2, body, 0)


def _band_bias_body(rev_ref, o_ref):
    tq, n = ATT_TILE, rev_ref.shape[-1]
    rolled = pltpu.roll(jnp.broadcast_to(rev_ref[0], (tq, n)), 1, 1, stride=1, stride_axis=0)
    q_chunk = lax.broadcasted_iota(jnp.int32, (tq, BAND_KEYS), 0) // CHUNK
    k_chunk = lax.broadcasted_iota(jnp.int32, (tq, BAND_KEYS), 1) // CHUNK
    in_band = (k_chunk >= q_chunk) & (k_chunk <= q_chunk + LEFT_CHUNKS)
    o_ref[0] = jnp.where(in_band, rolled[:, tq:tq + BAND_KEYS] * LOG2E, -jnp.inf)


def _band_bias_table(rel_bias):
    tq, n_heads = ATT_TILE, rel_bias.shape[0]
    shift = LEFT_CHUNKS * CHUNK
    assert shift - (BAND_KEYS - 1) >= -REL_CLIP and shift + tq - 1 >= REL_CLIP
    n_rel = tq - 1 + BAND_KEYS
    n_clipped = n_rel - (tq - 1 + REL_CLIP + 1)
    by_rel = jnp.concatenate(
        [rel_bias[:, REL_CLIP - (tq - 1):], jnp.broadcast_to(rel_bias[:, -1:], (n_heads, n_clipped))], axis=1)
    rev = jnp.pad(by_rel[:, ::-1], ((0, 0), (0, 1)))
    assert rev.shape[1] == tq + BAND_KEYS and shift + tq - 1 == n_rel - tq
    return pl.pallas_call(
        _band_bias_body,
        grid=(n_heads,),
        in_specs=[pl.BlockSpec((1, 1, rev.shape[1]), lambda h: (h, 0, 0))],
        out_specs=pl.BlockSpec((1, tq, BAND_KEYS), lambda h: (h, 0, 0)),
        out_shape=jax.ShapeDtypeStruct((n_heads, tq, BAND_KEYS), F32),
        compiler_params=_params(1, 8 * tq * rev.shape[1] * 4),
        name="band_bias",
    )(rev.reshape(n_heads, 1, rev.shape[1]))


def _band_attention(q, k, v, bias_table):
    bsz, seq, _ = q.shape
    g = BAND_HEADS_PER_STEP
    tq = ATT_TILE
    width = g * C_DIM
    seq_spec = pl.BlockSpec((1, seq, width), lambda b, hg: (b, 0, hg))
    nbytes = 2 * (4 * seq * width * 2 + g * tq * BAND_KEYS * 4) + 4 * g * tq * BAND_KEYS * 4
    return pl.pallas_call(
        _band_body,
        grid=(bsz, C_HEADS // g),
        in_specs=[seq_spec, seq_spec, seq_spec,
                  pl.BlockSpec((g, tq, BAND_KEYS), lambda b, hg: (hg, 0, 0))],
        out_specs=seq_spec,
        out_shape=jax.ShapeDtypeStruct((bsz, seq, C_HEADS * C_DIM), BF16),
        compiler_params=_params(2, nbytes),
        name="band_attention",
    )(q, k, v, bias_table)


def _mix_ffn_body(*refs, n_mix, final_norm):
    h_ref, gffn_ref = refs[0], refs[1]
    mix_refs = refs[2:2 + n_mix]
    wo_ref, wg_ref, wu_ref, wd_ref, gfin_ref, o_ref, acc_ref = refs[2 + n_mix:]
    mixed = jnp.concatenate([r[...] for r in mix_refs], axis=1) if n_mix > 1 else mix_refs[0][...]
    h = h_ref[...] + _dot(mixed, wo_ref[...])
    u = _rms(h, gffn_ref[...]).astype(BF16)
    d_ff = wg_ref.shape[1]
    acc_ref[...] = h
    for f0 in range(0, d_ff, FFN_COL_TILE):
        cols = slice(f0, f0 + FFN_COL_TILE)
        gate = _dot(u, wg_ref[:, cols])
        act = (gate * jax.nn.sigmoid(gate) * _dot(u, wu_ref[:, cols])).astype(BF16)
        acc_ref[...] += _dot(act, wd_ref[cols, :])
    out = acc_ref[...]
    o_ref[...] = _rms(out, gfin_ref[...]) if final_norm else out


def _mix_ffn(h, g_ffn, mix_outs, w_out, wg, wu, wd, g_final, final_norm):
    t, d = h.shape
    tm = FFN_ROW_TILE
    d_ff = wg.shape[1]
    assert d_ff % FFN_COL_TILE == 0
    row = lambda n: pl.BlockSpec((tm, n), lambda i: (i, 0))
    nbytes = ((3 * d * d_ff + w_out.size) * 2 + 2 * tm * d * (4 + 4) + 2 * tm * w_out.shape[0] * 2
              + 3 * tm * d * 4 + 4 * tm * FFN_COL_TILE * 4)
    return pl.pallas_call(
        functools.partial(_mix_ffn_body, n_mix=len(mix_outs), final_norm=final_norm),
        grid=(t // tm,),
        in_specs=[row(d), _resident(g_ffn.shape)] + [row(o.shape[1]) for o in mix_outs]
                 + [_resident(w_out.shape), _resident(wg.shape), _resident(wu.shape), _resident(wd.shape),
                    _resident(g_final.shape)],
        out_specs=row(d),
        out_shape=jax.ShapeDtypeStruct((t, d), F32),
        scratch_shapes=[pltpu.VMEM((tm, d), F32)],
        compiler_params=_params(1, nbytes),
        name="mix_ffn",
    )(h, g_ffn, *mix_outs, w_out, wg, wu, wd, g_final)


def _cast_body(*refs):
    n = len(refs) // 2
    for src, dst in zip(refs[:n], refs[n:]):
        dst[...] = src[0].astype(BF16)


def _cast_weights(stacks, indices):
    steps = WEIGHT_CAST_STEPS
    blocks = [(w.shape[1] // steps, w.shape[2]) for w in stacks]
    assert all(w.shape[1] % (steps * 16) == 0 for w in stacks)
    nbytes = 2 * sum(r * c * 6 for r, c in blocks)
    return pl.pallas_call(
        _cast_body,
        grid=(steps,),
        in_specs=[pl.BlockSpec((1, r, c), lambda i, l=l: (l, i, 0)) for (r, c), l in zip(blocks, indices)],
        out_specs=[pl.BlockSpec((r, c), lambda i: (i, 0)) for r, c in blocks],
        out_shape=[jax.ShapeDtypeStruct(w.shape[1:], BF16) for w in stacks],
        compiler_params=_params(1, nbytes),
        name="cast_weights",
    )(*stacks)


def _head_slots(w, n_heads, per_head, start, keep):
    k = w.shape[0]
    w = w.reshape(k, n_heads, per_head)[:, :, start:start + keep]
    return jnp.pad(w, ((0, 0), (0, 0), (0, LANES - keep))).reshape(k, n_heads * LANES)


def _rope_slot_tables(seq, scale):
    half = MLA_ROPE // 2
    pos = jnp.arange(seq, dtype=F32)
    inv_freq = ROPE_THETA ** (-jnp.arange(0, MLA_ROPE, 2, dtype=F32) / MLA_ROPE)
    ang = pos[:, None] * inv_freq[None, :]
    cos, sin = jnp.cos(ang), jnp.sin(ang)
    zeros = jnp.zeros((seq, half), F32)
    pad = jnp.zeros((seq, LANES - MLA_NOPE - MLA_ROPE), F32)
    ones = jnp.ones((seq, MLA_NOPE), F32)
    direct = jnp.concatenate([ones, cos, cos, pad], axis=1)
    from_lower = jnp.concatenate([0 * ones, zeros, sin, pad], axis=1)
    from_upper = jnp.concatenate([0 * ones, -sin, zeros, pad], axis=1)
    return jnp.stack([direct, from_lower, from_upper]) * scale


def kernel(x, ev_w_in, ev_g_cq, ev_w_uq, ev_g_ckv, ev_w_ukv, ev_w_out, od_w_qkv, od_rel_bias, od_w_out,
           g_mix, g_ffn, w_gate, w_up, w_down, g_final):
    bsz, seq, d = x.shape
    depth = g_mix.shape[0]
    t = bsz * seq
    h = x.reshape(t, d)
    g_fin = g_final.reshape(1, d)
    rope_q = _rope_slot_tables(seq, (MLA_NOPE + MLA_ROPE) ** -0.5 * LOG2E)
    rope_k = _rope_slot_tables(seq, 1.0)

    for layer in range(depth):
        i = layer // 2
        gm = g_mix[layer].reshape(1, d)
        if layer % 2 == 0:
            w_in = ev_w_in[i]
            o1, o2, o3 = Q_LORA, Q_LORA + KV_LORA, Q_LORA + KV_LORA + MLA_ROPE
            wcq = w_in[:, :o1].astype(BF16)
            wckv = w_in[:, o1:o2].astype(BF16)
            wkr = jnp.pad(w_in[:, o2:o3], ((0, 0), (MLA_NOPE, LANES - MLA_NOPE - MLA_ROPE))).astype(BF16)
            wb = w_in[:, o3:].astype(BF16)
            wuq = _head_slots(ev_w_uq[i], MLA_HEADS, MLA_NOPE + MLA_ROPE, 0, MLA_NOPE + MLA_ROPE).astype(BF16)
            wuk = _head_slots(ev_w_ukv[i], MLA_HEADS, MLA_NOPE + MLA_V, 0, MLA_NOPE).astype(BF16)
            wuv = _head_slots(ev_w_ukv[i], MLA_HEADS, MLA_NOPE + MLA_V, MLA_NOPE, MLA_V).astype(BF16)
            qa, ka, va, qb, kb, vb = _even_proj(
                h, gm, wcq, wckv, wkr, wb, ev_g_cq[i].reshape(1, -1), wuq, ev_g_ckv[i].reshape(1, -1),
                wuk, wuv, rope_q, rope_k, seq)
            r3 = lambda a: a.reshape(bsz, seq, a.shape[1])
            o_a = _mla_attention(r3(qa), r3(ka), r3(va)).reshape(t, -1)
            o_b = _sb_attention(r3(qb), r3(kb), r3(vb)).reshape(t, -1)
            mix_outs, w_out_stack = [o_a, o_b], ev_w_out
        else:
            (w_qkv,) = _cast_weights([od_w_qkv], [i])
            q, k, v = _odd_proj(h, gm, w_qkv)
            r3 = lambda a: a.reshape(bsz, seq, a.shape[1])
            o_c = _band_attention(r3(q), r3(k), r3(v), _band_bias_table(od_rel_bias[i])).reshape(t, -1)
            mix_outs, w_out_stack = [o_c], od_w_out
        w_out, wg, wu, wd = _cast_weights([w_out_stack, w_gate, w_up, w_down], [i, layer, layer, layer])
        h = _mix_ffn(h, g_ffn[layer].reshape(1, d), mix_outs, w_out, wg, wu, wd, g_fin,
                     final_norm=(layer == depth - 1))
    return h.reshape(bsz, seq, d)
```

```python
import functools

import jax
import jax.numpy as jnp
from jax import lax
from jax.experimental import pallas as pl
from jax.experimental.pallas import tpu as pltpu

F32 = jnp.float32
BF16 = jnp.bfloat16

CHUNK = 64
MLA_HEADS = 8
MLA_NOPE = 64
MLA_ROPE = 32
MLA_V = 64
Q_LORA = 384
KV_LORA = 256
ROPE_THETA = 10000.0
SB_HEADS = 8
SB_DIM = 64
C_HEADS = 16
C_DIM = 64
LEFT_CHUNKS = 8
REL_CLIP = 256
RMS_EPS = 1e-6

LANES = 128
V7X_VMEM_BYTES = 64 * 1024 * 1024

ROW_TILE = 512
FFN_ROW_TILE = 512
FFN_COL_TILE = 256
Q_TILE = 512
K_TILE = 256
HEADS_PER_STEP = 4
MLA_K_TILE = 512
MLA_DIAG_TILE = 256
MLA_HEADS_PER_STEP = 4
BAND_HEADS_PER_STEP = 8
ATT_TILE = 256
BAND_GROUP = ATT_TILE // CHUNK
BAND_KEYS = (LEFT_CHUNKS + BAND_GROUP) * CHUNK
WEIGHT_CAST_STEPS = 8

NEG_BIG = -1e30
LOG2E = 1.4426950408889634
EXP_UNDERFLOW = -104.0


def _vmem_limit(nbytes):
    return int(min(max(2 * nbytes, 32 * 1024 * 1024), V7X_VMEM_BYTES - 8 * 1024 * 1024))


def _params(n_axes, nbytes):
    return pltpu.CompilerParams(
        dimension_semantics=("arbitrary",) * n_axes,
        vmem_limit_bytes=_vmem_limit(nbytes),
    )


def _dot(a, b):
    return jnp.dot(a, b, preferred_element_type=F32)


def _dot_nt(a, b):
    return lax.dot_general(a, b, (((1,), (1,)), ((), ())), preferred_element_type=F32)


def _rms(x, g):
    return x * lax.rsqrt(jnp.mean(x * x, axis=-1, keepdims=True) + RMS_EPS) * g


def _resident(shape):
    nd = len(shape)
    return pl.BlockSpec(shape, lambda *_: (0,) * nd, pipeline_mode=pl.Buffered(1))


def _rope(x, tab_ref):
    return (x * tab_ref[0]
            + pltpu.roll(x, MLA_ROPE // 2, 1) * tab_ref[1]
            + pltpu.roll(x, LANES - MLA_ROPE // 2, 1) * tab_ref[2])


def _even_proj_body(h_ref, gmix_ref, wcq_ref, wckv_ref, wkr_ref, wb_ref, gcq_ref, wuq_ref,
                    gckv_ref, wuk_ref, wuv_ref, ropeq_ref, ropek_ref,
                    qa_ref, ka_ref, va_ref, qb_ref, kb_ref, vb_ref):
    u = _rms(h_ref[...], gmix_ref[...]).astype(BF16)
    nb = SB_HEADS * SB_DIM
    cq = _dot(u, wcq_ref[...])
    ckv = _dot(u, wckv_ref[...])
    qkb = _dot(u, wb_ref[:, :2 * nb])
    qb_ref[...] = (qkb[:, :nb] * (SB_DIM ** -0.5)).astype(BF16)
    kb_ref[...] = qkb[:, nb:].astype(BF16)
    cqn = _rms(cq, gcq_ref[...]).astype(BF16)
    ckvn = _rms(ckv, gckv_ref[...]).astype(BF16)
    va = _dot(ckvn, wuv_ref[...])
    ones_lane = lax.broadcasted_iota(jnp.int32, va.shape, 1) % LANES == MLA_V
    va_ref[...] = jnp.where(ones_lane, 1.0, va).astype(BF16)

    k_rope = _rope(_dot(u, wkr_ref[...]), ropek_ref)
    qa = _dot(cqn, wuq_ref[...])
    kn = _dot(ckvn, wuk_ref[...])
    for hh in range(MLA_HEADS):
        sl = slice(hh * LANES, (hh + 1) * LANES)
        qa_ref[:, sl] = _rope(qa[:, sl], ropeq_ref).astype(BF16)
        ka_ref[:, sl] = (kn[:, sl] + k_rope).astype(BF16)

    vb_ref[...] = _dot(u, wb_ref[:, 2 * nb:]).astype(BF16)


def _even_proj(h, gmix, wcq, wckv, wkr, wb, gcq, wuq, gckv, wuk, wuv, ropeq, ropek, seq):
    t, d = h.shape
    tm = ROW_TILE
    per_seq = seq // tm
    row = lambda n: pl.BlockSpec((tm, n), lambda i: (i, 0))
    rope_spec = pl.BlockSpec((3, tm, LANES), lambda i: (0, i % per_seq, 0))
    weights = (wcq, wckv, wkr, wb, gcq, wuq, gckv, wuk, wuv)
    out_widths = (MLA_HEADS * LANES,) * 3 + (SB_HEADS * SB_DIM,) * 3
    nbytes = (sum(w.size * w.dtype.itemsize for w in weights)
              + 2 * tm * d * 4 + 2 * tm * sum(out_widths) * 2 + tm * 8192 * 4)
    return pl.pallas_call(
        _even_proj_body,
        grid=(t // tm,),
        in_specs=[row(d), _resident(gmix.shape)]
                 + [_resident(wcq.shape), _resident(wckv.shape), _resident(wkr.shape), _resident(wb.shape),
                    _resident(gcq.shape), _resident(wuq.shape), _resident(gckv.shape), _resident(wuk.shape),
                    _resident(wuv.shape), rope_spec, rope_spec],
        out_specs=[row(n) for n in out_widths],
        out_shape=[jax.ShapeDtypeStruct((t, n), BF16) for n in out_widths],
        compiler_params=_params(1, nbytes),
        name="even_proj",
    )(h, gmix, wcq, wckv, wkr, wb, gcq, wuq, gckv, wuk, wuv, ropeq, ropek)


def _head_select(shape, head_in_pair):
    lane = lax.broadcasted_iota(jnp.int32, shape, 1)
    return (lane < SB_DIM) if head_in_pair == 0 else (lane >= SB_DIM)


def _mla_body(q_ref, k_ref, v_ref, o_ref, m_ref, acc_ref):
    tq, tk, HEADS_PER_STEP = Q_TILE, MLA_K_TILE, MLA_HEADS_PER_STEP
    assert tq == tk
    dk = MLA_DIAG_TILE
    qi = pl.program_id(2)
    row_chunk = lax.broadcasted_iota(jnp.int32, (tq, dk), 0) // CHUNK
    col_chunk = lax.broadcasted_iota(jnp.int32, (tq, dk), 1) // CHUNK
    slots = [slice(hh * LANES, (hh + 1) * LANES) for hh in range(HEADS_PER_STEP)]
    qs = [q_ref[0, :, sl] for sl in slots]

    def tile(k0, n_keys, visible, first_row=0):
        rows = pl.ds(pl.multiple_of(k0, n_keys), n_keys)
        heads = range(HEADS_PER_STEP)
        scores = [_dot_nt(qs[hh][first_row:], k_ref[0, rows, slots[hh]]) for hh in heads]
        probs = []
        for hh in heads:
            m = m_ref[hh, first_row:, :]
            s = scores[hh] if visible is None else jnp.where(visible, scores[hh], -jnp.inf)
            m_new = jnp.maximum(m, jnp.max(s, axis=-1, keepdims=True))
            p = [jnp.exp2(s[:, j:j + LANES] - m_new) for j in range(0, n_keys, LANES)]
            probs.append(jnp.concatenate(p, axis=1).astype(BF16))
            m_ref[hh, first_row:, :] = m_new
            acc_ref[hh, first_row:, :] *= jnp.exp2(m - m_new)
        for hh in heads:
            acc_ref[hh, first_row:, :] += _dot(probs[hh], v_ref[0, rows, slots[hh]])

    m_ref[...] = jnp.full(m_ref.shape, NEG_BIG, F32)
    acc_ref[...] = jnp.zeros(acc_ref.shape, F32)

    def full_tile(kb, _):
        tile(kb * tk, tk, None)
        return 0

    lax.fori_loop(0, qi, full_tile, 0)
    for d in range(tq // dk):
        first = d * dk
        tile(qi * tq + first, dk, (col_chunk <= row_chunk)[:tq - first], first_row=first)
    outs = [acc_ref[hh] / acc_ref[hh, :, MLA_V:MLA_V + 1] for hh in range(HEADS_PER_STEP)]
    lane = lax.broadcasted_iota(jnp.int32, (tq, LANES), 1)
    for p in range(HEADS_PER_STEP // 2):
        packed = jnp.where(lane < MLA_V, outs[2 * p], pltpu.roll(outs[2 * p + 1], MLA_V, 1))
        o_ref[0, :, p * LANES:(p + 1) * LANES] = packed.astype(BF16)


def _mla_attention(qa, ka, va):
    bsz, seq, _ = qa.shape
    tq, g = Q_TILE, MLA_HEADS_PER_STEP
    nbytes = 2 * (2 * seq * g * LANES * 2 + tq * g * LANES * 3) + 8 * g * tq * MLA_K_TILE * 4
    return pl.pallas_call(
        _mla_body,
        grid=(bsz, MLA_HEADS // g, seq // tq),
        in_specs=[pl.BlockSpec((1, tq, g * LANES), lambda b, hg, qi: (b, qi, hg)),
                  pl.BlockSpec((1, seq, g * LANES), lambda b, hg, qi: (b, 0, hg)),
                  pl.BlockSpec((1, seq, g * LANES), lambda b, hg, qi: (b, 0, hg))],
        out_specs=pl.BlockSpec((1, tq, g * MLA_V), lambda b, hg, qi: (b, qi, hg)),
        out_shape=jax.ShapeDtypeStruct((bsz, seq, MLA_HEADS * MLA_V), BF16),
        scratch_shapes=[pltpu.VMEM((g, tq, LANES), F32), pltpu.VMEM((g, tq, LANES), F32)],
        compiler_params=_params(3, nbytes),
        name="mla_attention",
    )(qa, ka, va)


def _sb_body(q_ref, k_ref, v_ref, ntri_ref, o_ref, later_ref, acc_ref):
    tq, tk = Q_TILE, K_TILE
    qi = pl.program_id(2)
    per = tq // tk
    row = lax.broadcasted_iota(jnp.int32, (tq, tk), 0)
    col = lax.broadcasted_iota(jnp.int32, (tq, tk), 1)
    pair_slots = [slice(p * LANES, (p + 1) * LANES) for p in range(HEADS_PER_STEP // 2)]
    qs = []
    for hh in range(HEADS_PER_STEP):
        q_pair = q_ref[0, :, pair_slots[hh // 2]]
        qs.append(jnp.where(_head_select((tq, LANES), hh % 2), q_pair, jnp.zeros_like(q_pair)))

    def tile(kb, before, first_row=0):
        rows = pl.ds(pl.multiple_of(kb * tk, tk), tk)
        heads = range(HEADS_PER_STEP)
        zs = [_dot_nt(qs[hh][first_row:], k_ref[0, rows, pair_slots[hh // 2]]) for hh in heads]
        softplus = []
        for hh in heads:
            sp = jnp.maximum(zs[hh], 0.0) + jnp.log(1.0 + jnp.exp2(jnp.abs(zs[hh]) * -LOG2E))
            softplus.append(sp if before is None else jnp.where(before, sp, 0.0))
        cums = []
        for hh in heads:
            local = _dot(softplus[hh].astype(BF16), ntri_ref[...])
            later = later_ref[hh, first_row:, :]
            cums.append(jnp.concatenate([local[:, j:j + LANES] + later for j in range(0, tk, LANES)], axis=1))
        weights = []
        for hh in heads:
            w = jnp.exp((zs[hh] - softplus[hh]) + cums[hh])
            if before is not None:
                w = jnp.where(before, w, 0.0)
            weights.append(w.astype(BF16))
            later_ref[hh, first_row:, :] = jnp.broadcast_to(
                cums[hh][:, 0:1] - softplus[hh][:, 0:1], (tq - first_row, LANES))
        for hh in heads:
            acc_ref[hh, first_row:, :] += _dot(weights[hh], v_ref[0, rows, pair_slots[hh // 2]])

    later_ref[...] = jnp.zeros(later_ref.shape, F32)
    acc_ref[...] = jnp.zeros(acc_ref.shape, F32)
    for d in reversed(range(per)):
        first = d * tk
        tile(per * qi + d, (col < row)[:tq - first], first_row=first)

    n_earlier = per * qi

    def any_weight_left():
        return (jnp.max(later_ref[...]) > EXP_UNDERFLOW).astype(jnp.int32)

    def earlier_tile(state):
        tile(n_earlier - 1 - state[0], None)
        return state[0] + 1, any_weight_left()

    lax.while_loop(lambda state: jnp.logical_and(state[0] < n_earlier, state[1] > 0),
                   earlier_tile, (jnp.int32(0), any_weight_left()))
    for p in range(HEADS_PER_STEP // 2):
        packed = jnp.where(_head_select((tq, LANES), 0), acc_ref[2 * p], acc_ref[2 * p + 1])
        o_ref[0, :, pair_slots[p]] = packed.astype(BF16)


def _sb_attention(qb, kb, vb):
    bsz, seq, _ = qb.shape
    tq, tk, g = Q_TILE, K_TILE, HEADS_PER_STEP
    width = g * SB_DIM
    j = jnp.arange(tk)
    ntri2 = -(j[:, None] > j[None, :]).astype(BF16)
    nbytes = 2 * (2 * seq * width * 2 + 2 * tq * width * 2) + tk * tk * 2 + 8 * g * tq * tk * 4
    return pl.pallas_call(
        _sb_body,
        grid=(bsz, SB_HEADS // g, seq // tq),
        in_specs=[pl.BlockSpec((1, tq, width), lambda b, hg, qi: (b, qi, hg)),
                  pl.BlockSpec((1, seq, width), lambda b, hg, qi: (b, 0, hg)),
                  pl.BlockSpec((1, seq, width), lambda b, hg, qi: (b, 0, hg)),
                  _resident(ntri2.shape)],
        out_specs=pl.BlockSpec((1, tq, width), lambda b, hg, qi: (b, qi, hg)),
        out_shape=jax.ShapeDtypeStruct((bsz, seq, SB_HEADS * SB_DIM), BF16),
        scratch_shapes=[pltpu.VMEM((g, tq, LANES), F32), pltpu.VMEM((g, tq, LANES), F32)],
        compiler_params=_params(3, nbytes),
        name="sb_attention",
    )(qb, kb, vb, ntri2)


def _odd_proj_body(h_ref, gmix_ref, w_ref, q_ref, k_ref, v_ref):
    u = _rms(h_ref[...], gmix_ref[...]).astype(BF16)
    n = C_HEADS * C_DIM
    qkv = _dot(u, w_ref[...])
    q_ref[...] = (qkv[:, :n] * (C_DIM ** -0.5 * LOG2E)).astype(BF16)
    k_ref[...] = qkv[:, n:2 * n].astype(BF16)
    v_ref[...] = qkv[:, 2 * n:].astype(BF16)


def _odd_proj(h, gmix, w_qkv):
    t, d = h.shape
    tm = ROW_TILE
    n = C_HEADS * C_DIM
    row = lambda m: pl.BlockSpec((tm, m), lambda i: (i, 0))
    nbytes = w_qkv.size * 2 + 2 * tm * d * 4 + 2 * 3 * tm * n * 2 + 2 * tm * 3 * n * 4
    return pl.pallas_call(
        _odd_proj_body,
        grid=(t // tm,),
        in_specs=[row(d), _resident(gmix.shape), _resident(w_qkv.shape)],
        out_specs=[row(n)] * 3,
        out_shape=[jax.ShapeDtypeStruct((t, n), BF16)] * 3,
        compiler_params=_params(1, nbytes),
        name="odd_proj",
    )(h, gmix, w_qkv)


def _band_body(q_ref, k_ref, v_ref, bias_ref, o_ref):
    tq = ATT_TILE

    def group(q0, k0, n_keys, bias_off):
        heads = range(BAND_HEADS_PER_STEP)
        pair_slots = [slice(p * LANES, (p + 1) * LANES) for p in range(BAND_HEADS_PER_STEP // 2)]
        scores = []
        for hh in heads:
            q_pair = q_ref[0, pl.ds(q0, tq), pair_slots[hh // 2]]
            q = jnp.where(_head_select((tq, LANES), hh % 2), q_pair, jnp.zeros_like(q_pair))
            scores.append(_dot_nt(q, k_ref[0, pl.ds(k0, n_keys), pair_slots[hh // 2]]))
        probs = []
        for hh in heads:
            s = scores[hh] + bias_ref[hh, :, bias_off:bias_off + n_keys]
            probs.append(jnp.exp2(s - jnp.max(s, axis=-1, keepdims=True)).astype(BF16))
        outs = []
        for hh in heads:
            v_pair = v_ref[0, pl.ds(k0, n_keys), pair_slots[hh // 2]]
            vals = jnp.where(_head_select((n_keys, LANES), hh % 2), v_pair, jnp.ones_like(v_pair))
            pv = _dot(probs[hh], vals)
            row_sum_lane = (1 - hh % 2) * C_DIM
            outs.append(pv / pv[:, row_sum_lane:row_sum_lane + 1])
        for p in range(BAND_HEADS_PER_STEP // 2):
            packed = jnp.where(_head_select((tq, LANES), 0), outs[2 * p], outs[2 * p + 1])
            o_ref[0, pl.ds(q0, tq), pair_slots[p]] = packed.astype(BF16)

    lead = LEFT_CHUNKS // BAND_GROUP
    for g in range(lead):
        group(g * tq, 0, (g + 1) * tq, LEFT_CHUNKS * CHUNK - g * tq)

    def body(i, _):
        for half in range(2):
            q0 = pl.multiple_of((lead + 2 * i + half) * tq, tq)
            group(q0, pl.multiple_of(q0 - LEFT_CHUNKS * CHUNK, tq), BAND_KEYS, 0)
        return 0

    n_groups = q_ref.shape[1] // tq - lead
    assert n_groups % 2 == 0
    lax.fori_loop(0, n_groups // 2, body, 0)


def _band_bias_body(rev_ref, o_ref):
    tq, n = ATT_TILE, rev_ref.shape[-1]
    rolled = pltpu.roll(jnp.broadcast_to(rev_ref[0], (tq, n)), 1, 1, stride=1, stride_axis=0)
    q_chunk = lax.broadcasted_iota(jnp.int32, (tq, BAND_KEYS), 0) // CHUNK
    k_chunk = lax.broadcasted_iota(jnp.int32, (tq, BAND_KEYS), 1) // CHUNK
    in_band = (k_chunk >= q_chunk) & (k_chunk <= q_chunk + LEFT_CHUNKS)
    o_ref[0] = jnp.where(in_band, rolled[:, tq:tq + BAND_KEYS] * LOG2E, -jnp.inf)


def _band_bias_table(rel_bias):
    tq, n_heads = ATT_TILE, rel_bias.shape[0]
    shift = LEFT_CHUNKS * CHUNK
    assert shift - (BAND_KEYS - 1) >= -REL_CLIP and shift + tq - 1 >= REL_CLIP
    n_rel = tq - 1 + BAND_KEYS
    n_clipped = n_rel - (tq - 1 + REL_CLIP + 1)
    by_rel = jnp.concatenate(
        [rel_bias[:, REL_CLIP - (tq - 1):], jnp.broadcast_to(rel_bias[:, -1:], (n_heads, n_clipped))], axis=1)
    rev = jnp.pad(by_rel[:, ::-1], ((0, 0), (0, 1)))
    assert rev.shape[1] == tq + BAND_KEYS and shift + tq - 1 == n_rel - tq
    return pl.pallas_call(
        _band_bias_body,
        grid=(n_heads,),
        in_specs=[pl.BlockSpec((1, 1, rev.shape[1]), lambda h: (h, 0, 0))],
        out_specs=pl.BlockSpec((1, tq, BAND_KEYS), lambda h: (h, 0, 0)),
        out_shape=jax.ShapeDtypeStruct((n_heads, tq, BAND_KEYS), F32),
        compiler_params=_params(1, 8 * tq * rev.shape[1] * 4),
        name="band_bias",
    )(rev.reshape(n_heads, 1, rev.shape[1]))


def _band_attention(q, k, v, bias_table):
    bsz, seq, _ = q.shape
    g = BAND_HEADS_PER_STEP
    tq = ATT_TILE
    width = g * C_DIM
    seq_spec = pl.BlockSpec((1, seq, width), lambda b, hg: (b, 0, hg))
    nbytes = 2 * (4 * seq * width * 2 + g * tq * BAND_KEYS * 4) + 4 * g * tq * BAND_KEYS * 4
    return pl.pallas_call(
        _band_body,
        grid=(bsz, C_HEADS // g),
        in_specs=[seq_spec, seq_spec, seq_spec,
                  pl.BlockSpec((g, tq, BAND_KEYS), lambda b, hg: (hg, 0, 0))],
        out_specs=seq_spec,
        out_shape=jax.ShapeDtypeStruct((bsz, seq, C_HEADS * C_DIM), BF16),
        compiler_params=_params(2, nbytes),
        name="band_attention",
    )(q, k, v, bias_table)


def _mix_ffn_body(*refs, n_mix, final_norm):
    h_ref, gffn_ref = refs[0], refs[1]
    mix_refs = refs[2:2 + n_mix]
    wo_ref, wg_ref, wu_ref, wd_ref, gfin_ref, o_ref, acc_ref = refs[2 + n_mix:]
    mixed = jnp.concatenate([r[...] for r in mix_refs], axis=1) if n_mix > 1 else mix_refs[0][...]
    h = h_ref[...] + _dot(mixed, wo_ref[...])
    u = _rms(h, gffn_ref[...]).astype(BF16)
    d_ff = wg_ref.shape[1]
    acc_ref[...] = h
    for f0 in range(0, d_ff, FFN_COL_TILE):
        cols = slice(f0, f0 + FFN_COL_TILE)
        gate = _dot(u, wg_ref[:, cols])
        act = (gate * jax.nn.sigmoid(gate) * _dot(u, wu_ref[:, cols])).astype(BF16)
        acc_ref[...] += _dot(act, wd_ref[cols, :])
    out = acc_ref[...]
    o_ref[...] = _rms(out, gfin_ref[...]) if final_norm else out


def _mix_ffn(h, g_ffn, mix_outs, w_out, wg, wu, wd, g_final, final_norm):
    t, d = h.shape
    tm = FFN_ROW_TILE
    d_ff = wg.shape[1]
    assert d_ff % FFN_COL_TILE == 0
    row = lambda n: pl.BlockSpec((tm, n), lambda i: (i, 0))
    nbytes = ((3 * d * d_ff + w_out.size) * 2 + 2 * tm * d * (4 + 4) + 2 * tm * w_out.shape[0] * 2
              + 3 * tm * d * 4 + 4 * tm * FFN_COL_TILE * 4)
    return pl.pallas_call(
        functools.partial(_mix_ffn_body, n_mix=len(mix_outs), final_norm=final_norm),
        grid=(t // tm,),
        in_specs=[row(d), _resident(g_ffn.shape)] + [row(o.shape[1]) for o in mix_outs]
                 + [_resident(w_out.shape), _resident(wg.shape), _resident(wu.shape), _resident(wd.shape),
                    _resident(g_final.shape)],
        out_specs=row(d),
        out_shape=jax.ShapeDtypeStruct((t, d), F32),
        scratch_shapes=[pltpu.VMEM((tm, d), F32)],
        compiler_params=_params(1, nbytes),
        name="mix_ffn",
    )(h, g_ffn, *mix_outs, w_out, wg, wu, wd, g_final)


def _cast_body(*refs):
    n = len(refs) // 2
    for src, dst in zip(refs[:n], refs[n:]):
        dst[...] = src[0].astype(BF16)


def _cast_weights(stacks, indices):
    steps = WEIGHT_CAST_STEPS
    blocks = [(w.shape[1] // steps, w.shape[2]) for w in stacks]
    assert all(w.shape[1] % (steps * 16) == 0 for w in stacks)
    nbytes = 2 * sum(r * c * 6 for r, c in blocks)
    return pl.pallas_call(
        _cast_body,
        grid=(steps,),
        in_specs=[pl.BlockSpec((1, r, c), lambda i, l=l: (l, i, 0)) for (r, c), l in zip(blocks, indices)],
        out_specs=[pl.BlockSpec((r, c), lambda i: (i, 0)) for r, c in blocks],
        out_shape=[jax.ShapeDtypeStruct(w.shape[1:], BF16) for w in stacks],
        compiler_params=_params(1, nbytes),
        name="cast_weights",
    )(*stacks)


def _head_slots(w, n_heads, per_head, start, keep):
    k = w.shape[0]
    w = w.reshape(k, n_heads, per_head)[:, :, start:start + keep]
    return jnp.pad(w, ((0, 0), (0, 0), (0, LANES - keep))).reshape(k, n_heads * LANES)


def _rope_slot_tables(seq, scale):
    half = MLA_ROPE // 2
    pos = jnp.arange(seq, dtype=F32)
    inv_freq = ROPE_THETA ** (-jnp.arange(0, MLA_ROPE, 2, dtype=F32) / MLA_ROPE)
    ang = pos[:, None] * inv_freq[None, :]
    cos, sin = jnp.cos(ang), jnp.sin(ang)
    zeros = jnp.zeros((seq, half), F32)
    pad = jnp.zeros((seq, LANES - MLA_NOPE - MLA_ROPE), F32)
    ones = jnp.ones((seq, MLA_NOPE), F32)
    direct = jnp.concatenate([ones, cos, cos, pad], axis=1)
    from_lower = jnp.concatenate([0 * ones, zeros, sin, pad], axis=1)
    from_upper = jnp.concatenate([0 * ones, -sin, zeros, pad], axis=1)
    return jnp.stack([direct, from_lower, from_upper]) * scale


def kernel(x, ev_w_in, ev_g_cq, ev_w_uq, ev_g_ckv, ev_w_ukv, ev_w_out, od_w_qkv, od_rel_bias, od_w_out,
           g_mix, g_ffn, w_gate, w_up, w_down, g_final):
    bsz, seq, d = x.shape
    depth = g_mix.shape[0]
    t = bsz * seq
    h = x.reshape(t, d)
    g_fin = g_final.reshape(1, d)
    rope_q = _rope_slot_tables(seq, (MLA_NOPE + MLA_ROPE) ** -0.5 * LOG2E)
    rope_k = _rope_slot_tables(seq, 1.0)

    for layer in range(depth):
        i = layer // 2
        gm = g_mix[layer].reshape(1, d)
        if layer % 2 == 0:
            w_in = ev_w_in[i]
            o1, o2, o3 = Q_LORA, Q_LORA + KV_LORA, Q_LORA + KV_LORA + MLA_ROPE
            wcq = w_in[:, :o1].astype(BF16)
            wckv = w_in[:, o1:o2].astype(BF16)
            wkr = jnp.pad(w_in[:, o2:o3], ((0, 0), (MLA_NOPE, LANES - MLA_NOPE - MLA_ROPE))).astype(BF16)
            wb = w_in[:, o3:].astype(BF16)
            wuq = _head_slots(ev_w_uq[i], MLA_HEADS, MLA_NOPE + MLA_ROPE, 0, MLA_NOPE + MLA_ROPE).astype(BF16)
            wuk = _head_slots(ev_w_ukv[i], MLA_HEADS, MLA_NOPE + MLA_V, 0, MLA_NOPE).astype(BF16)
            wuv = _head_slots(ev_w_ukv[i], MLA_HEADS, MLA_NOPE + MLA_V, MLA_NOPE, MLA_V).astype(BF16)
            qa, ka, va, qb, kb, vb = _even_proj(
                h, gm, wcq, wckv, wkr, wb, ev_g_cq[i].reshape(1, -1), wuq, ev_g_ckv[i].reshape(1, -1),
                wuk, wuv, rope_q, rope_k, seq)
            r3 = lambda a: a.reshape(bsz, seq, a.shape[1])
            o_a = _mla_attention(r3(qa), r3(ka), r3(va)).reshape(t, -1)
            o_b = _sb_attention(r3(qb), r3(kb), r3(vb)).reshape(t, -1)
            mix_outs, w_out_stack = [o_a, o_b], ev_w_out
        else:
            (w_qkv,) = _cast_weights([od_w_qkv], [i])
            q, k, v = _odd_proj(h, gm, w_qkv)
            r3 = lambda a: a.reshape(bsz, seq, a.shape[1])
            o_c = _band_attention(r3(q), r3(k), r3(v), _band_bias_table(od_rel_bias[i])).reshape(t, -1)
            mix_outs, w_out_stack = [o_c], od_w_out
        w_out, wg, wu, wd = _cast_weights([w_out_stack, w_gate, w_up, w_down], [i, layer, layer, layer])
        h = _mix_ffn(h, g_ffn[layer].reshape(1, d), mix_outs, w_out, wg, wu, wd, g_fin,
                     final_norm=(layer == depth - 1))
    return h.reshape(bsz, seq, d)
```

```python
import functools

import jax
import jax.numpy as jnp
from jax import lax
from jax.experimental import pallas as pl
from jax.experimental.pallas import tpu as pltpu

F32 = jnp.float32
BF16 = jnp.bfloat16

CHUNK = 64
MLA_HEADS = 8
MLA_NOPE = 64
MLA_ROPE = 32
MLA_V = 64
Q_LORA = 384
KV_LORA = 256
ROPE_THETA = 10000.0
SB_HEADS = 8
SB_DIM = 64
C_HEADS = 16
C_DIM = 64
LEFT_CHUNKS = 8
REL_CLIP = 256
RMS_EPS = 1e-6

LANES = 128
V7X_VMEM_BYTES = 64 * 1024 * 1024

ROW_TILE = 512
FFN_ROW_TILE = 512
FFN_COL_TILE = 256
Q_TILE = 512
K_TILE = 256
HEADS_PER_STEP = 4
MLA_K_TILE = 512
MLA_DIAG_TILE = 256
MLA_HEADS_PER_STEP = 4
BAND_HEADS_PER_STEP = 8
ATT_TILE = 256
BAND_GROUP = ATT_TILE // CHUNK
BAND_KEYS = (LEFT_CHUNKS + BAND_GROUP) * CHUNK
WEIGHT_CAST_STEPS = 8

SB_HEAD_ROWS = 192
LOG2E = 1.4426950408889634
EXP_UNDERFLOW = -104.0


def _vmem_limit(nbytes):
    return int(min(max(2 * nbytes, 32 * 1024 * 1024), V7X_VMEM_BYTES - 8 * 1024 * 1024))


def _params(n_axes, nbytes):
    return pltpu.CompilerParams(
        dimension_semantics=("arbitrary",) * n_axes,
        vmem_limit_bytes=_vmem_limit(nbytes),
    )


def _dot(a, b):
    return jnp.dot(a, b, preferred_element_type=F32)


def _dot_nt(a, b):
    return lax.dot_general(a, b, (((1,), (1,)), ((), ())), preferred_element_type=F32)


def _rms(x, g):
    return x * lax.rsqrt(jnp.mean(x * x, axis=-1, keepdims=True) + RMS_EPS) * g


def _resident(shape):
    nd = len(shape)
    return pl.BlockSpec(shape, lambda *_: (0,) * nd, pipeline_mode=pl.Buffered(1))


def _rope(x, tab_ref):
    return (x * tab_ref[0]
            + pltpu.roll(x, MLA_ROPE // 2, 1) * tab_ref[1]
            + pltpu.roll(x, LANES - MLA_ROPE // 2, 1) * tab_ref[2])


def _even_proj_body(h_ref, gmix_ref, wcq_ref, wckv_ref, wkr_ref, wb_ref, gcq_ref, wuq_ref,
                    gckv_ref, wuk_ref, wuv_ref, ropeq_ref, ropek_ref,
                    qa_ref, ka_ref, va_ref, qb_ref, kb_ref, vb_ref):
    u = _rms(h_ref[...], gmix_ref[...]).astype(BF16)
    nb = SB_HEADS * SB_DIM
    cq = _dot(u, wcq_ref[...])
    ckv = _dot(u, wckv_ref[...])
    qkb = _dot(u, wb_ref[:, :2 * nb])
    qb_ref[...] = (qkb[:, :nb] * (SB_DIM ** -0.5)).astype(BF16)
    kb_ref[...] = qkb[:, nb:].astype(BF16)
    cqn = _rms(cq, gcq_ref[...]).astype(BF16)
    ckvn = _rms(ckv, gckv_ref[...]).astype(BF16)
    va = _dot(ckvn, wuv_ref[...])
    ones_lane = lax.broadcasted_iota(jnp.int32, va.shape, 1) % LANES == MLA_V
    va_ref[...] = jnp.where(ones_lane, 1.0, va).astype(BF16)

    k_rope = _rope(_dot(u, wkr_ref[...]), ropek_ref)
    qa = _dot(cqn, wuq_ref[...])
    kn = _dot(ckvn, wuk_ref[...])
    for hh in range(MLA_HEADS):
        sl = slice(hh * LANES, (hh + 1) * LANES)
        qa_ref[:, sl] = _rope(qa[:, sl], ropeq_ref).astype(BF16)
        ka_ref[:, sl] = (kn[:, sl] + k_rope).astype(BF16)

    vb_ref[...] = _dot(u, wb_ref[:, 2 * nb:]).astype(BF16)


def _even_proj(h, gmix, wcq, wckv, wkr, wb, gcq, wuq, gckv, wuk, wuv, ropeq, ropek, seq):
    t, d = h.shape
    tm = ROW_TILE
    per_seq = seq // tm
    row = lambda n: pl.BlockSpec((tm, n), lambda i: (i, 0))
    rope_spec = pl.BlockSpec((3, tm, LANES), lambda i: (0, i % per_seq, 0))
    weights = (wcq, wckv, wkr, wb, gcq, wuq, gckv, wuk, wuv)
    out_widths = (MLA_HEADS * LANES,) * 3 + (SB_HEADS * SB_DIM,) * 3
    nbytes = (sum(w.size * w.dtype.itemsize for w in weights)
              + 2 * tm * d * 4 + 2 * tm * sum(out_widths) * 2 + tm * 8192 * 4)
    return pl.pallas_call(
        _even_proj_body,
        grid=(t // tm,),
        in_specs=[row(d), _resident(gmix.shape)]
                 + [_resident(wcq.shape), _resident(wckv.shape), _resident(wkr.shape), _resident(wb.shape),
                    _resident(gcq.shape), _resident(wuq.shape), _resident(gckv.shape), _resident(wuk.shape),
                    _resident(wuv.shape), rope_spec, rope_spec],
        out_specs=[row(n) for n in out_widths],
        out_shape=[jax.ShapeDtypeStruct((t, n), BF16) for n in out_widths],
        compiler_params=_params(1, nbytes),
        name="even_proj",
    )(h, gmix, wcq, wckv, wkr, wb, gcq, wuq, gckv, wuk, wuv, ropeq, ropek)


def _head_select(shape, head_in_pair):
    lane = lax.broadcasted_iota(jnp.int32, shape, 1)
    return (lane < SB_DIM) if head_in_pair == 0 else (lane >= SB_DIM)


def _mla_body(q_ref, k_ref, v_ref, o_ref, m_ref, acc_ref):
    tq, tk, HEADS_PER_STEP = Q_TILE, MLA_K_TILE, MLA_HEADS_PER_STEP
    assert tq == tk
    dk = MLA_DIAG_TILE
    qi = pl.program_id(2)
    row_chunk = lax.broadcasted_iota(jnp.int32, (tq, dk), 0) // CHUNK
    col_chunk = lax.broadcasted_iota(jnp.int32, (tq, dk), 1) // CHUNK
    slots = [slice(hh * LANES, (hh + 1) * LANES) for hh in range(HEADS_PER_STEP)]
    qs = [q_ref[0, :, sl] for sl in slots]

    def tile(k0, n_keys, visible, first_row=0, first_visit=False):
        rows = pl.ds(pl.multiple_of(k0, n_keys), n_keys)
        heads = range(HEADS_PER_STEP)
        scores = [_dot_nt(qs[hh][first_row:], k_ref[0, rows, slots[hh]]) for hh in heads]
        probs = []
        for hh in heads:
            s = scores[hh] if visible is None else jnp.where(visible, scores[hh], -jnp.inf)
            m_new = jnp.max(s, axis=-1, keepdims=True)
            if first_visit:
                m_new = jnp.broadcast_to(m_new, (tq - first_row, LANES))
            else:
                m = m_ref[hh, first_row:, :]
                m_new = jnp.maximum(m, m_new)
                acc_ref[hh, first_row:, :] *= jnp.exp2(m - m_new)
            p = [jnp.exp2(s[:, j:j + LANES] - m_new) for j in range(0, n_keys, LANES)]
            probs.append(jnp.concatenate(p, axis=1).astype(BF16))
            m_ref[hh, first_row:, :] = m_new
        for hh in heads:
            pv = _dot(probs[hh], v_ref[0, rows, slots[hh]])
            if first_visit:
                acc_ref[hh, first_row:, :] = pv
            else:
                acc_ref[hh, first_row:, :] += pv

    for d in range(tq // dk):
        first = d * dk
        tile(qi * tq + first, dk, (col_chunk <= row_chunk)[:tq - first], first_row=first, first_visit=(d == 0))

    def full_tile(kb, _):
        tile(kb * tk, tk, None)
        return 0

    lax.fori_loop(0, qi, full_tile, 0)
    outs = [acc_ref[hh] / acc_ref[hh, :, MLA_V:MLA_V + 1] for hh in range(HEADS_PER_STEP)]
    lane = lax.broadcasted_iota(jnp.int32, (tq, LANES), 1)
    for p in range(HEADS_PER_STEP // 2):
        packed = jnp.where(lane < MLA_V, outs[2 * p], pltpu.roll(outs[2 * p + 1], MLA_V, 1))
        o_ref[0, :, p * LANES:(p + 1) * LANES] = packed.astype(BF16)


def _mla_attention(qa, ka, va):
    bsz, seq, _ = qa.shape
    tq, g = Q_TILE, MLA_HEADS_PER_STEP
    nbytes = 2 * (2 * seq * g * LANES * 2 + tq * g * LANES * 3) + 8 * g * tq * MLA_K_TILE * 4
    return pl.pallas_call(
        _mla_body,
        grid=(bsz, MLA_HEADS // g, seq // tq),
        in_specs=[pl.BlockSpec((1, tq, g * LANES), lambda b, hg, qi: (b, qi, hg)),
                  pl.BlockSpec((1, seq, g * LANES), lambda b, hg, qi: (b, 0, hg)),
                  pl.BlockSpec((1, seq, g * LANES), lambda b, hg, qi: (b, 0, hg))],
        out_specs=pl.BlockSpec((1, tq, g * MLA_V), lambda b, hg, qi: (b, qi, hg)),
        out_shape=jax.ShapeDtypeStruct((bsz, seq, MLA_HEADS * MLA_V), BF16),
        scratch_shapes=[pltpu.VMEM((g, tq, LANES), F32), pltpu.VMEM((g, tq, LANES), F32)],
        compiler_params=_params(3, nbytes),
        name="mla_attention",
    )(qa, ka, va)


def _sb_body(q_ref, k_ref, v_ref, ntri_ref, o_ref, later_ref, acc_ref):
    tq, tk = Q_TILE, K_TILE
    qi = pl.program_id(2)
    per = tq // tk
    row = lax.broadcasted_iota(jnp.int32, (tq, tk), 0)
    col = lax.broadcasted_iota(jnp.int32, (tq, tk), 1)
    pair_slots = [slice(p * LANES, (p + 1) * LANES) for p in range(HEADS_PER_STEP // 2)]
    qs = []
    for hh in range(HEADS_PER_STEP):
        q_pair = q_ref[0, :, pair_slots[hh // 2]]
        qs.append(jnp.where(_head_select((tq, LANES), hh % 2), q_pair, jnp.zeros_like(q_pair)))

    def tile(kb, before, first_row=0, last_row=Q_TILE):
        rows = pl.ds(pl.multiple_of(kb * tk, tk), tk)
        heads = range(HEADS_PER_STEP)
        zs = [_dot_nt(qs[hh][first_row:last_row], k_ref[0, rows, pair_slots[hh // 2]]) for hh in heads]
        softplus = []
        for hh in heads:
            sp = jnp.maximum(zs[hh], 0.0) + jnp.log(1.0 + jnp.exp2(jnp.abs(zs[hh]) * -LOG2E))
            softplus.append(sp if before is None else jnp.where(before, sp, 0.0))
        cums = []
        for hh in heads:
            local = _dot(softplus[hh].astype(BF16), ntri_ref[...])
            later = later_ref[hh, first_row:last_row, :]
            cums.append(jnp.concatenate([local[:, j:j + LANES] + later for j in range(0, tk, LANES)], axis=1))
        weights, largest = [], None
        for hh in heads:
            w = jnp.exp((zs[hh] - softplus[hh]) + cums[hh])
            if before is not None:
                w = jnp.where(before, w, 0.0)
            weights.append(w.astype(BF16))
            later = cums[hh][:, 0:1] - softplus[hh][:, 0:1]
            later_ref[hh, first_row:last_row, :] = jnp.broadcast_to(later, (last_row - first_row, LANES))
            largest = later if largest is None else jnp.maximum(largest, later)
        for hh in heads:
            acc_ref[hh, first_row:last_row, :] += _dot(weights[hh], v_ref[0, rows, pair_slots[hh // 2]])
        if first_row:
            return None
        rest = jnp.max(largest[SB_HEAD_ROWS:]) if last_row > SB_HEAD_ROWS else None
        return jnp.max(largest[:SB_HEAD_ROWS]), rest

    later_ref[...] = jnp.zeros(later_ref.shape, F32)
    acc_ref[...] = jnp.zeros(acc_ref.shape, F32)
    for d in reversed(range(per)):
        first = d * tk
        largest_sums = tile(per * qi + d, (col < row)[:tq - first], first_row=first)

    n_earlier = per * qi

    def earlier_tile(state):
        i, head_sum, rest_sum = state
        kb = n_earlier - 1 - i
        head_sum, rest_sum = lax.cond(
            rest_sum <= EXP_UNDERFLOW,
            lambda: (tile(kb, None, last_row=SB_HEAD_ROWS)[0], rest_sum),
            lambda: tile(kb, None))
        return i + 1, head_sum, rest_sum

    lax.while_loop(
        lambda state: jnp.logical_and(state[0] < n_earlier, jnp.maximum(state[1], state[2]) > EXP_UNDERFLOW),
        earlier_tile, (jnp.int32(0),) + largest_sums)
    for p in range(HEADS_PER_STEP // 2):
        packed = jnp.where(_head_select((tq, LANES), 0), acc_ref[2 * p], acc_ref[2 * p + 1])
        o_ref[0, :, pair_slots[p]] = packed.astype(BF16)


def _sb_attention(qb, kb, vb):
    bsz, seq, _ = qb.shape
    tq, tk, g = Q_TILE, K_TILE, HEADS_PER_STEP
    width = g * SB_DIM
    j = jnp.arange(tk)
    ntri2 = -(j[:, None] > j[None, :]).astype(BF16)
    nbytes = 2 * (2 * seq * width * 2 + 2 * tq * width * 2) + tk * tk * 2 + 8 * g * tq * tk * 4
    return pl.pallas_call(
        _sb_body,
        grid=(bsz, SB_HEADS // g, seq // tq),
        in_specs=[pl.BlockSpec((1, tq, width), lambda b, hg, qi: (b, qi, hg)),
                  pl.BlockSpec((1, seq, width), lambda b, hg, qi: (b, 0, hg)),
                  pl.BlockSpec((1, seq, width), lambda b, hg, qi: (b, 0, hg)),
                  _resident(ntri2.shape)],
        out_specs=pl.BlockSpec((1, tq, width), lambda b, hg, qi: (b, qi, hg)),
        out_shape=jax.ShapeDtypeStruct((bsz, seq, SB_HEADS * SB_DIM), BF16),
        scratch_shapes=[pltpu.VMEM((g, tq, LANES), F32), pltpu.VMEM((g, tq, LANES), F32)],
        compiler_params=_params(3, nbytes),
        name="sb_attention",
    )(qb, kb, vb, ntri2)


def _odd_proj_body(h_ref, gmix_ref, w_ref, q_ref, k_ref, v_ref):
    u = _rms(h_ref[...], gmix_ref[...]).astype(BF16)
    n = C_HEADS * C_DIM
    qkv = _dot(u, w_ref[...])
    q_ref[...] = (qkv[:, :n] * (C_DIM ** -0.5 * LOG2E)).astype(BF16)
    k_ref[...] = qkv[:, n:2 * n].astype(BF16)
    v_ref[...] = qkv[:, 2 * n:].astype(BF16)


def _odd_proj(h, gmix, w_qkv):
    t, d = h.shape
    tm = ROW_TILE
    n = C_HEADS * C_DIM
    row = lambda m: pl.BlockSpec((tm, m), lambda i: (i, 0))
    nbytes = w_qkv.size * 2 + 2 * tm * d * 4 + 2 * 3 * tm * n * 2 + 2 * tm * 3 * n * 4
    return pl.pallas_call(
        _odd_proj_body,
        grid=(t // tm,),
        in_specs=[row(d), _resident(gmix.shape), _resident(w_qkv.shape)],
        out_specs=[row(n)] * 3,
        out_shape=[jax.ShapeDtypeStruct((t, n), BF16)] * 3,
        compiler_params=_params(1, nbytes),
        name="odd_proj",
    )(h, gmix, w_qkv)


def _band_body(q_ref, k_ref, v_ref, bias_ref, o_ref):
    tq = ATT_TILE

    def group(q0, k0, n_keys, bias_off):
        heads = range(BAND_HEADS_PER_STEP)
        pair_slots = [slice(p * LANES, (p + 1) * LANES) for p in range(BAND_HEADS_PER_STEP // 2)]
        scores = []
        for hh in heads:
            q_pair = q_ref[0, pl.ds(q0, tq), pair_slots[hh // 2]]
            q = jnp.where(_head_select((tq, LANES), hh % 2), q_pair, jnp.zeros_like(q_pair))
            scores.append(_dot_nt(q, k_ref[0, pl.ds(k0, n_keys), pair_slots[hh // 2]]))
        probs = []
        for hh in heads:
            s = scores[hh] + bias_ref[hh, :, bias_off:bias_off + n_keys]
            probs.append(jnp.exp2(s - jnp.max(s, axis=-1, keepdims=True)).astype(BF16))
        outs = []
        for hh in heads:
            v_pair = v_ref[0, pl.ds(k0, n_keys), pair_slots[hh // 2]]
            vals = jnp.where(_head_select((n_keys, LANES), hh % 2), v_pair, jnp.ones_like(v_pair))
            pv = _dot(probs[hh], vals)
            row_sum_lane = (1 - hh % 2) * C_DIM
            outs.append(pv / pv[:, row_sum_lane:row_sum_lane + 1])
        for p in range(BAND_HEADS_PER_STEP // 2):
            packed = jnp.where(_head_select((tq, LANES), 0), outs[2 * p], outs[2 * p + 1])
            o_ref[0, pl.ds(q0, tq), pair_slots[p]] = packed.astype(BF16)

    lead = LEFT_CHUNKS // BAND_GROUP
    for g in range(lead):
        group(g * tq, 0, (g + 1) * tq, LEFT_CHUNKS * CHUNK - g * tq)

    def body(i, _):
        for half in range(2):
            q0 = pl.multiple_of((lead + 2 * i + half) * tq, tq)
            group(q0, pl.multiple_of(q0 - LEFT_CHUNKS * CHUNK, tq), BAND_KEYS, 0)
        return 0

    n_groups = q_ref.shape[1] // tq - lead
    assert n_groups % 2 == 0
    lax.fori_loop(0, n_groups // 2, body, 0)


def _band_bias_body(rev_ref, o_ref):
    tq, n = ATT_TILE, rev_ref.shape[-1]
    rolled = pltpu.roll(jnp.broadcast_to(rev_ref[0], (tq, n)), 1, 1, stride=1, stride_axis=0)
    q_chunk = lax.broadcasted_iota(jnp.int32, (tq, BAND_KEYS), 0) // CHUNK
    k_chunk = lax.broadcasted_iota(jnp.int32, (tq, BAND_KEYS), 1) // CHUNK
    in_band = (k_chunk >= q_chunk) & (k_chunk <= q_chunk + LEFT_CHUNKS)
    o_ref[0] = jnp.where(in_band, rolled[:, tq:tq + BAND_KEYS] * LOG2E, -jnp.inf)


def _band_bias_table(rel_bias):
    tq, n_heads = ATT_TILE, rel_bias.shape[0]
    shift = LEFT_CHUNKS * CHUNK
    assert shift - (BAND_KEYS - 1) >= -REL_CLIP and shift + tq - 1 >= REL_CLIP
    n_rel = tq - 1 + BAND_KEYS
    n_clipped = n_rel - (tq - 1 + REL_CLIP + 1)
    by_rel = jnp.concatenate(
        [rel_bias[:, REL_CLIP - (tq - 1):], jnp.broadcast_to(rel_bias[:, -1:], (n_heads, n_clipped))], axis=1)
    rev = jnp.pad(by_rel[:, ::-1], ((0, 0), (0, 1)))
    assert rev.shape[1] == tq + BAND_KEYS and shift + tq - 1 == n_rel - tq
    return pl.pallas_call(
        _band_bias_body,
        grid=(n_heads,),
        in_specs=[pl.BlockSpec((1, 1, rev.shape[1]), lambda h: (h, 0, 0))],
        out_specs=pl.BlockSpec((1, tq, BAND_KEYS), lambda h: (h, 0, 0)),
        out_shape=jax.ShapeDtypeStruct((n_heads, tq, BAND_KEYS), F32),
        compiler_params=_params(1, 8 * tq * rev.shape[1] * 4),
        name="band_bias",
    )(rev.reshape(n_heads, 1, rev.shape[1]))


def _band_attention(q, k, v, bias_table):
    bsz, seq, _ = q.shape
    g = BAND_HEADS_PER_STEP
    tq = ATT_TILE
    width = g * C_DIM
    seq_spec = pl.BlockSpec((1, seq, width), lambda b, hg: (b, 0, hg))
    nbytes = 2 * (4 * seq * width * 2 + g * tq * BAND_KEYS * 4) + 4 * g * tq * BAND_KEYS * 4
    return pl.pallas_call(
        _band_body,
        grid=(bsz, C_HEADS // g),
        in_specs=[seq_spec, seq_spec, seq_spec,
                  pl.BlockSpec((g, tq, BAND_KEYS), lambda b, hg: (hg, 0, 0))],
        out_specs=seq_spec,
        out_shape=jax.ShapeDtypeStruct((bsz, seq, C_HEADS * C_DIM), BF16),
        compiler_params=_params(2, nbytes),
        name="band_attention",
    )(q, k, v, bias_table)


def _mix_ffn_body(*refs, n_mix, final_norm):
    h_ref, gffn_ref = refs[0], refs[1]
    mix_refs = refs[2:2 + n_mix]
    wo_ref, wg_ref, wu_ref, wd_ref, gfin_ref, o_ref, acc_ref = refs[2 + n_mix:]
    mixed = jnp.concatenate([r[...] for r in mix_refs], axis=1) if n_mix > 1 else mix_refs[0][...]
    h = h_ref[...] + _dot(mixed, wo_ref[...])
    u = _rms(h, gffn_ref[...]).astype(BF16)
    d_ff = wg_ref.shape[1]
    acc_ref[...] = h
    for f0 in range(0, d_ff, FFN_COL_TILE):
        cols = slice(f0, f0 + FFN_COL_TILE)
        gate = _dot(u, wg_ref[:, cols])
        act = (gate * jax.nn.sigmoid(gate) * _dot(u, wu_ref[:, cols])).astype(BF16)
        acc_ref[...] += _dot(act, wd_ref[cols, :])
    out = acc_ref[...]
    o_ref[...] = _rms(out, gfin_ref[...]) if final_norm else out


def _mix_ffn(h, g_ffn, mix_outs, w_out, wg, wu, wd, g_final, final_norm):
    t, d = h.shape
    tm = FFN_ROW_TILE
    d_ff = wg.shape[1]
    assert d_ff % FFN_COL_TILE == 0
    row = lambda n: pl.BlockSpec((tm, n), lambda i: (i, 0))
    nbytes = ((3 * d * d_ff + w_out.size) * 2 + 2 * tm * d * (4 + 4) + 2 * tm * w_out.shape[0] * 2
              + 3 * tm * d * 4 + 4 * tm * FFN_COL_TILE * 4)
    return pl.pallas_call(
        functools.partial(_mix_ffn_body, n_mix=len(mix_outs), final_norm=final_norm),
        grid=(t // tm,),
        in_specs=[row(d), _resident(g_ffn.shape)] + [row(o.shape[1]) for o in mix_outs]
                 + [_resident(w_out.shape), _resident(wg.shape), _resident(wu.shape), _resident(wd.shape),
                    _resident(g_final.shape)],
        out_specs=row(d),
        out_shape=jax.ShapeDtypeStruct((t, d), F32),
        scratch_shapes=[pltpu.VMEM((tm, d), F32)],
        compiler_params=_params(1, nbytes),
        name="mix_ffn",
    )(h, g_ffn, *mix_outs, w_out, wg, wu, wd, g_final)


def _cast_body(*refs):
    n = len(refs) // 2
    for src, dst in zip(refs[:n], refs[n:]):
        dst[...] = src[0].astype(BF16)


def _cast_weights(stacks, indices):
    steps = WEIGHT_CAST_STEPS
    blocks = [(w.shape[1] // steps, w.shape[2]) for w in stacks]
    assert all(w.shape[1] % (steps * 16) == 0 for w in stacks)
    nbytes = 2 * sum(r * c * 6 for r, c in blocks)
    return pl.pallas_call(
        _cast_body,
        grid=(steps,),
        in_specs=[pl.BlockSpec((1, r, c), lambda i, l=l: (l, i, 0)) for (r, c), l in zip(blocks, indices)],
        out_specs=[pl.BlockSpec((r, c), lambda i: (i, 0)) for r, c in blocks],
        out_shape=[jax.ShapeDtypeStruct(w.shape[1:], BF16) for w in stacks],
        compiler_params=_params(1, nbytes),
        name="cast_weights",
    )(*stacks)


def _head_slots(w, n_heads, per_head, start, keep):
    k = w.shape[0]
    w = w.reshape(k, n_heads, per_head)[:, :, start:start + keep]
    return jnp.pad(w, ((0, 0), (0, 0), (0, LANES - keep))).reshape(k, n_heads * LANES)


def _rope_slot_tables(seq, scale):
    half = MLA_ROPE // 2
    pos = jnp.arange(seq, dtype=F32)
    inv_freq = ROPE_THETA ** (-jnp.arange(0, MLA_ROPE, 2, dtype=F32) / MLA_ROPE)
    ang = pos[:, None] * inv_freq[None, :]
    cos, sin = jnp.cos(ang), jnp.sin(ang)
    zeros = jnp.zeros((seq, half), F32)
    pad = jnp.zeros((seq, LANES - MLA_NOPE - MLA_ROPE), F32)
    ones = jnp.ones((seq, MLA_NOPE), F32)
    direct = jnp.concatenate([ones, cos, cos, pad], axis=1)
    from_lower = jnp.concatenate([0 * ones, zeros, sin, pad], axis=1)
    from_upper = jnp.concatenate([0 * ones, -sin, zeros, pad], axis=1)
    return jnp.stack([direct, from_lower, from_upper]) * scale


def kernel(x, ev_w_in, ev_g_cq, ev_w_uq, ev_g_ckv, ev_w_ukv, ev_w_out, od_w_qkv, od_rel_bias, od_w_out,
           g_mix, g_ffn, w_gate, w_up, w_down, g_final):
    bsz, seq, d = x.shape
    depth = g_mix.shape[0]
    t = bsz * seq
    h = x.reshape(t, d)
    g_fin = g_final.reshape(1, d)
    rope_q = _rope_slot_tables(seq, (MLA_NOPE + MLA_ROPE) ** -0.5 * LOG2E)
    rope_k = _rope_slot_tables(seq, 1.0)

    for layer in range(depth):
        i = layer // 2
        gm = g_mix[layer].reshape(1, d)
        if layer % 2 == 0:
            w_in = ev_w_in[i]
            o1, o2, o3 = Q_LORA, Q_LORA + KV_LORA, Q_LORA + KV_LORA + MLA_ROPE
            wcq = w_in[:, :o1].astype(BF16)
            wckv = w_in[:, o1:o2].astype(BF16)
            wkr = jnp.pad(w_in[:, o2:o3], ((0, 0), (MLA_NOPE, LANES - MLA_NOPE - MLA_ROPE))).astype(BF16)
            wb = w_in[:, o3:].astype(BF16)
            wuq = _head_slots(ev_w_uq[i], MLA_HEADS, MLA_NOPE + MLA_ROPE, 0, MLA_NOPE + MLA_ROPE).astype(BF16)
            wuk = _head_slots(ev_w_ukv[i], MLA_HEADS, MLA_NOPE + MLA_V, 0, MLA_NOPE).astype(BF16)
            wuv = _head_slots(ev_w_ukv[i], MLA_HEADS, MLA_NOPE + MLA_V, MLA_NOPE, MLA_V).astype(BF16)
            qa, ka, va, qb, kb, vb = _even_proj(
                h, gm, wcq, wckv, wkr, wb, ev_g_cq[i].reshape(1, -1), wuq, ev_g_ckv[i].reshape(1, -1),
                wuk, wuv, rope_q, rope_k, seq)
            r3 = lambda a: a.reshape(bsz, seq, a.shape[1])
            o_a = _mla_attention(r3(qa), r3(ka), r3(va)).reshape(t, -1)
            o_b = _sb_attention(r3(qb), r3(kb), r3(vb)).reshape(t, -1)
            mix_outs, w_out_stack = [o_a, o_b], ev_w_out
        else:
            (w_qkv,) = _cast_weights([od_w_qkv], [i])
            q, k, v = _odd_proj(h, gm, w_qkv)
            r3 = lambda a: a.reshape(bsz, seq, a.shape[1])
            o_c = _band_attention(r3(q), r3(k), r3(v), _band_bias_table(od_rel_bias[i])).reshape(t, -1)
            mix_outs, w_out_stack = [o_c], od_w_out
        w_out, wg, wu, wd = _cast_weights([w_out_stack, w_gate, w_up, w_down], [i, layer, layer, layer])
        h = _mix_ffn(h, g_ffn[layer].reshape(1, d), mix_outs, w_out, wg, wu, wd, g_fin,
                     final_norm=(layer == depth - 1))
    return h.reshape(bsz, seq, d)
```

```python
import functools

import jax
import jax.numpy as jnp
import numpy as np
from jax import lax
from jax.experimental import pallas as pl
from jax.experimental.pallas import tpu as pltpu

F32 = jnp.float32
BF16 = jnp.bfloat16

CHUNK = 64
MLA_HEADS = 8
MLA_NOPE = 64
MLA_ROPE = 32
MLA_V = 64
Q_LORA = 384
KV_LORA = 256
ROPE_THETA = 10000.0
SB_HEADS = 8
SB_DIM = 64
C_HEADS = 16
C_DIM = 64
LEFT_CHUNKS = 8
REL_CLIP = 256
RMS_EPS = 1e-6

LANES = 128
V7X_VMEM_BYTES = 64 * 1024 * 1024

ROW_TILE = 512
FFN_ROW_TILE = 512
FFN_COL_TILE = 256
Q_TILE = 512
K_TILE = 256
HEADS_PER_STEP = 8
MLA_K_TILE = 512
MLA_DIAG_TILE = 256
MLA_HEADS_PER_STEP = 8
BAND_HEADS_PER_STEP = 8
ATT_TILE = 256
BAND_GROUP = ATT_TILE // CHUNK
BAND_KEYS = (LEFT_CHUNKS + BAND_GROUP) * CHUNK
WEIGHT_CAST_STEPS = 8

SB_HEAD_ROWS = 192
LOG2E = 1.4426950408889634
EXP_UNDERFLOW = -104.0


def _vmem_limit(nbytes):
    return int(min(max(2 * nbytes, 32 * 1024 * 1024), V7X_VMEM_BYTES - 8 * 1024 * 1024))


def _params(n_axes, nbytes):
    return pltpu.CompilerParams(
        dimension_semantics=("arbitrary",) * n_axes,
        vmem_limit_bytes=_vmem_limit(nbytes),
    )


def _dot(a, b):
    return jnp.dot(a, b, preferred_element_type=F32)


def _dot_nt(a, b):
    return lax.dot_general(a, b, (((1,), (1,)), ((), ())), preferred_element_type=F32)


def _rms(x, g):
    return x * lax.rsqrt(jnp.mean(x * x, axis=-1, keepdims=True) + RMS_EPS) * g


def _resident(shape):
    nd = len(shape)
    return pl.BlockSpec(shape, lambda *_: (0,) * nd, pipeline_mode=pl.Buffered(1))


def _rope(x, tab_ref):
    return (x * tab_ref[0]
            + pltpu.roll(x, MLA_ROPE // 2, 1) * tab_ref[1]
            + pltpu.roll(x, LANES - MLA_ROPE // 2, 1) * tab_ref[2])


def _even_proj_body(h_ref, gmix_ref, wcq_ref, wckv_ref, wkr_ref, wb_ref, gcq_ref, wuq_ref,
                    gckv_ref, wuk_ref, wuv_ref, ropeq_ref, ropek_ref,
                    qa_ref, ka_ref, va_ref, qb_ref, kb_ref, vb_ref):
    u = _rms(h_ref[...], gmix_ref[...]).astype(BF16)
    nb = SB_HEADS * SB_DIM
    cq = _dot(u, wcq_ref[...])
    ckv = _dot(u, wckv_ref[...])
    qkb = _dot(u, wb_ref[:, :2 * nb])
    qb_ref[...] = (qkb[:, :nb] * (SB_DIM ** -0.5)).astype(BF16)
    kb_ref[...] = qkb[:, nb:].astype(BF16)
    cqn = _rms(cq, gcq_ref[...]).astype(BF16)
    ckvn = _rms(ckv, gckv_ref[...]).astype(BF16)
    va = _dot(ckvn, wuv_ref[...])
    ones_lane = lax.broadcasted_iota(jnp.int32, va.shape, 1) % LANES == MLA_V
    va_ref[...] = jnp.where(ones_lane, 1.0, va).astype(BF16)

    k_rope = _rope(_dot(u, wkr_ref[...]), ropek_ref)
    qa = _dot(cqn, wuq_ref[...])
    kn = _dot(ckvn, wuk_ref[...])
    for hh in range(MLA_HEADS):
        sl = slice(hh * LANES, (hh + 1) * LANES)
        qa_ref[:, sl] = _rope(qa[:, sl], ropeq_ref).astype(BF16)
        ka_ref[:, sl] = (kn[:, sl] + k_rope).astype(BF16)

    vb_ref[...] = _dot(u, wb_ref[:, 2 * nb:]).astype(BF16)


def _even_proj(h, gmix, wcq, wckv, wkr, wb, gcq, wuq, gckv, wuk, wuv, ropeq, ropek, seq):
    t, d = h.shape
    tm = ROW_TILE
    per_seq = seq // tm
    row = lambda n: pl.BlockSpec((tm, n), lambda i: (i, 0))
    rope_spec = pl.BlockSpec((3, tm, LANES), lambda i: (0, i % per_seq, 0))
    weights = (wcq, wckv, wkr, wb, gcq, wuq, gckv, wuk, wuv)
    out_widths = (MLA_HEADS * LANES,) * 3 + (SB_HEADS * SB_DIM,) * 3
    nbytes = (sum(w.size * w.dtype.itemsize for w in weights)
              + 2 * tm * d * 4 + 2 * tm * sum(out_widths) * 2 + tm * 8192 * 4)
    return pl.pallas_call(
        _even_proj_body,
        grid=(t // tm,),
        in_specs=[row(d), _resident(gmix.shape)]
                 + [_resident(wcq.shape), _resident(wckv.shape), _resident(wkr.shape), _resident(wb.shape),
                    _resident(gcq.shape), _resident(wuq.shape), _resident(gckv.shape), _resident(wuk.shape),
                    _resident(wuv.shape), rope_spec, rope_spec],
        out_specs=[row(n) for n in out_widths],
        out_shape=[jax.ShapeDtypeStruct((t, n), BF16) for n in out_widths],
        compiler_params=_params(1, nbytes),
        name="even_proj",
    )(h, gmix, wcq, wckv, wkr, wb, gcq, wuq, gckv, wuk, wuv, ropeq, ropek)


def _head_select(shape, head_in_pair):
    lane = lax.broadcasted_iota(jnp.int32, shape, 1)
    return (lane < SB_DIM) if head_in_pair == 0 else (lane >= SB_DIM)


def _mla_body(q_ref, k_ref, v_ref, o_ref, m_ref, acc_ref):
    tq, tk, HEADS_PER_STEP = Q_TILE, MLA_K_TILE, MLA_HEADS_PER_STEP
    assert tq == tk
    dk = MLA_DIAG_TILE
    qi = pl.program_id(2)
    row_chunk = lax.broadcasted_iota(jnp.int32, (tq, dk), 0) // CHUNK
    col_chunk = lax.broadcasted_iota(jnp.int32, (tq, dk), 1) // CHUNK
    slots = [slice(hh * LANES, (hh + 1) * LANES) for hh in range(HEADS_PER_STEP)]
    qs = [q_ref[0, :, sl] for sl in slots]

    def tile(k0, n_keys, visible, first_row=0, first_visit=False):
        rows = pl.ds(pl.multiple_of(k0, n_keys), n_keys)
        heads = range(HEADS_PER_STEP)
        scores = [_dot_nt(qs[hh][first_row:], k_ref[0, rows, slots[hh]]) for hh in heads]
        probs = []
        for hh in heads:
            s = scores[hh] if visible is None else jnp.where(visible, scores[hh], -jnp.inf)
            m_new = jnp.max(s, axis=-1, keepdims=True)
            if first_visit:
                m_new = jnp.broadcast_to(m_new, (tq - first_row, LANES))
            else:
                m = m_ref[hh, first_row:, :]
                m_new = jnp.maximum(m, m_new)
                acc_ref[hh, first_row:, :] *= jnp.exp2(m - m_new)
            p = [jnp.exp2(s[:, j:j + LANES] - m_new) for j in range(0, n_keys, LANES)]
            probs.append(jnp.concatenate(p, axis=1).astype(BF16))
            m_ref[hh, first_row:, :] = m_new
        for hh in heads:
            pv = _dot(probs[hh], v_ref[0, rows, slots[hh]])
            if first_visit:
                acc_ref[hh, first_row:, :] = pv
            else:
                acc_ref[hh, first_row:, :] += pv

    for d in range(tq // dk):
        first = d * dk
        tile(qi * tq + first, dk, (col_chunk <= row_chunk)[:tq - first], first_row=first, first_visit=(d == 0))

    def full_tile(kb, _):
        tile(kb * tk, tk, None)
        return 0

    lax.fori_loop(0, qi, full_tile, 0)
    outs = [acc_ref[hh] / acc_ref[hh, :, MLA_V:MLA_V + 1] for hh in range(HEADS_PER_STEP)]
    lane = lax.broadcasted_iota(jnp.int32, (tq, LANES), 1)
    for p in range(HEADS_PER_STEP // 2):
        packed = jnp.where(lane < MLA_V, outs[2 * p], pltpu.roll(outs[2 * p + 1], MLA_V, 1))
        o_ref[0, :, p * LANES:(p + 1) * LANES] = packed.astype(BF16)


def _mla_attention(qa, ka, va):
    bsz, seq, _ = qa.shape
    tq, g = Q_TILE, MLA_HEADS_PER_STEP
    nbytes = 2 * (2 * seq * g * LANES * 2 + tq * g * LANES * 3) + 8 * g * tq * MLA_K_TILE * 4
    return pl.pallas_call(
        _mla_body,
        grid=(bsz, MLA_HEADS // g, seq // tq),
        in_specs=[pl.BlockSpec((1, tq, g * LANES), lambda b, hg, qi: (b, qi, hg)),
                  pl.BlockSpec((1, seq, g * LANES), lambda b, hg, qi: (b, 0, hg)),
                  pl.BlockSpec((1, seq, g * LANES), lambda b, hg, qi: (b, 0, hg))],
        out_specs=pl.BlockSpec((1, tq, g * MLA_V), lambda b, hg, qi: (b, qi, hg)),
        out_shape=jax.ShapeDtypeStruct((bsz, seq, MLA_HEADS * MLA_V), BF16),
        scratch_shapes=[pltpu.VMEM((g, tq, LANES), F32), pltpu.VMEM((g, tq, LANES), F32)],
        compiler_params=_params(3, nbytes),
        name="mla_attention",
    )(qa, ka, va)


def _sb_body(q_ref, k_ref, v_ref, ntri_ref, o_ref, later_ref, acc_ref):
    tq, tk = Q_TILE, K_TILE
    qi = pl.program_id(2)
    per = tq // tk
    row = lax.broadcasted_iota(jnp.int32, (tq, tk), 0)
    col = lax.broadcasted_iota(jnp.int32, (tq, tk), 1)
    pair_slots = [slice(p * LANES, (p + 1) * LANES) for p in range(HEADS_PER_STEP // 2)]
    qs = []
    for hh in range(HEADS_PER_STEP):
        q_pair = q_ref[0, :, pair_slots[hh // 2]]
        qs.append(jnp.where(_head_select((tq, LANES), hh % 2), q_pair, jnp.zeros_like(q_pair)))

    def tile(kb, before, first_row=0, last_row=Q_TILE):
        rows = pl.ds(pl.multiple_of(kb * tk, tk), tk)
        heads = range(HEADS_PER_STEP)
        zs = [_dot_nt(qs[hh][first_row:last_row], k_ref[0, rows, pair_slots[hh // 2]]) for hh in heads]
        softplus = []
        for hh in heads:
            sp = jnp.maximum(zs[hh], 0.0) + jnp.log(1.0 + jnp.exp2(jnp.abs(zs[hh]) * -LOG2E))
            softplus.append(sp if before is None else jnp.where(before, sp, 0.0))
        cums = []
        for hh in heads:
            local = _dot(softplus[hh].astype(BF16), ntri_ref[...])
            later = later_ref[hh, first_row:last_row, :]
            cums.append(jnp.concatenate([local[:, j:j + LANES] + later for j in range(0, tk, LANES)], axis=1))
        weights, largest = [], None
        for hh in heads:
            w = jnp.exp((zs[hh] - softplus[hh]) + cums[hh])
            if before is not None:
                w = jnp.where(before, w, 0.0)
            weights.append(w.astype(BF16))
            later = cums[hh][:, 0:1] - softplus[hh][:, 0:1]
            later_ref[hh, first_row:last_row, :] = jnp.broadcast_to(later, (last_row - first_row, LANES))
            largest = later if largest is None else jnp.maximum(largest, later)
        for hh in heads:
            acc_ref[hh, first_row:last_row, :] += _dot(weights[hh], v_ref[0, rows, pair_slots[hh // 2]])
        if first_row:
            return None
        rest = jnp.max(largest[SB_HEAD_ROWS:]) if last_row > SB_HEAD_ROWS else None
        return jnp.max(largest[:SB_HEAD_ROWS]), rest

    later_ref[...] = jnp.zeros(later_ref.shape, F32)
    acc_ref[...] = jnp.zeros(acc_ref.shape, F32)
    for d in reversed(range(per)):
        first = d * tk
        largest_sums = tile(per * qi + d, (col < row)[:tq - first], first_row=first)

    n_earlier = per * qi

    def earlier_tile(state):
        i, head_sum, rest_sum = state
        kb = n_earlier - 1 - i
        head_sum, rest_sum = lax.cond(
            rest_sum <= EXP_UNDERFLOW,
            lambda: (tile(kb, None, last_row=SB_HEAD_ROWS)[0], rest_sum),
            lambda: tile(kb, None))
        return i + 1, head_sum, rest_sum

    lax.while_loop(
        lambda state: jnp.logical_and(state[0] < n_earlier, jnp.maximum(state[1], state[2]) > EXP_UNDERFLOW),
        earlier_tile, (jnp.int32(0),) + largest_sums)
    for p in range(HEADS_PER_STEP // 2):
        packed = jnp.where(_head_select((tq, LANES), 0), acc_ref[2 * p], acc_ref[2 * p + 1])
        o_ref[0, :, pair_slots[p]] = packed.astype(BF16)


def _sb_attention(qb, kb, vb):
    bsz, seq, _ = qb.shape
    tq, tk, g = Q_TILE, K_TILE, HEADS_PER_STEP
    width = g * SB_DIM
    j = np.arange(tk)
    ntri2 = jnp.asarray(np.where(j[:, None] > j[None, :], -1.0, 0.0), dtype=BF16)
    nbytes = 2 * (2 * seq * width * 2 + 2 * tq * width * 2) + tk * tk * 2 + 8 * g * tq * tk * 4
    return pl.pallas_call(
        _sb_body,
        grid=(bsz, SB_HEADS // g, seq // tq),
        in_specs=[pl.BlockSpec((1, tq, width), lambda b, hg, qi: (b, qi, hg)),
                  pl.BlockSpec((1, seq, width), lambda b, hg, qi: (b, 0, hg)),
                  pl.BlockSpec((1, seq, width), lambda b, hg, qi: (b, 0, hg)),
                  _resident(ntri2.shape)],
        out_specs=pl.BlockSpec((1, tq, width), lambda b, hg, qi: (b, qi, hg)),
        out_shape=jax.ShapeDtypeStruct((bsz, seq, SB_HEADS * SB_DIM), BF16),
        scratch_shapes=[pltpu.VMEM((g, tq, LANES), F32), pltpu.VMEM((g, tq, LANES), F32)],
        compiler_params=_params(3, nbytes),
        name="sb_attention",
    )(qb, kb, vb, ntri2)


def _odd_proj_body(h_ref, gmix_ref, w_ref, q_ref, k_ref, v_ref):
    u = _rms(h_ref[...], gmix_ref[...]).astype(BF16)
    n = C_HEADS * C_DIM
    qkv = _dot(u, w_ref[...])
    q_ref[...] = (qkv[:, :n] * (C_DIM ** -0.5 * LOG2E)).astype(BF16)
    k_ref[...] = qkv[:, n:2 * n].astype(BF16)
    v_ref[...] = qkv[:, 2 * n:].astype(BF16)


def _odd_proj(h, gmix, w_qkv):
    t, d = h.shape
    tm = ROW_TILE
    n = C_HEADS * C_DIM
    row = lambda m: pl.BlockSpec((tm, m), lambda i: (i, 0))
    nbytes = w_qkv.size * 2 + 2 * tm * d * 4 + 2 * 3 * tm * n * 2 + 2 * tm * 3 * n * 4
    return pl.pallas_call(
        _odd_proj_body,
        grid=(t // tm,),
        in_specs=[row(d), _resident(gmix.shape), _resident(w_qkv.shape)],
        out_specs=[row(n)] * 3,
        out_shape=[jax.ShapeDtypeStruct((t, n), BF16)] * 3,
        compiler_params=_params(1, nbytes),
        name="odd_proj",
    )(h, gmix, w_qkv)


def _band_body(q_ref, k_ref, v_ref, bias_ref, o_ref):
    tq = ATT_TILE

    def group(q0, k0, n_keys, bias_off):
        heads = range(BAND_HEADS_PER_STEP)
        pair_slots = [slice(p * LANES, (p + 1) * LANES) for p in range(BAND_HEADS_PER_STEP // 2)]
        scores = []
        for hh in heads:
            q_pair = q_ref[0, pl.ds(q0, tq), pair_slots[hh // 2]]
            q = jnp.where(_head_select((tq, LANES), hh % 2), q_pair, jnp.zeros_like(q_pair))
            scores.append(_dot_nt(q, k_ref[0, pl.ds(k0, n_keys), pair_slots[hh // 2]]))
        probs = []
        for hh in heads:
            s = scores[hh] + bias_ref[hh, :, bias_off:bias_off + n_keys]
            probs.append(jnp.exp2(s - jnp.max(s, axis=-1, keepdims=True)).astype(BF16))
        outs = []
        for hh in heads:
            v_pair = v_ref[0, pl.ds(k0, n_keys), pair_slots[hh // 2]]
            vals = jnp.where(_head_select((n_keys, LANES), hh % 2), v_pair, jnp.ones_like(v_pair))
            pv = _dot(probs[hh], vals)
            row_sum_lane = (1 - hh % 2) * C_DIM
            outs.append(pv / pv[:, row_sum_lane:row_sum_lane + 1])
        for p in range(BAND_HEADS_PER_STEP // 2):
            packed = jnp.where(_head_select((tq, LANES), 0), outs[2 * p], outs[2 * p + 1])
            o_ref[0, pl.ds(q0, tq), pair_slots[p]] = packed.astype(BF16)

    lead = LEFT_CHUNKS // BAND_GROUP
    for g in range(lead):
        group(g * tq, 0, (g + 1) * tq, LEFT_CHUNKS * CHUNK - g * tq)

    def body(i, _):
        for half in range(2):
            q0 = pl.multiple_of((lead + 2 * i + half) * tq, tq)
            group(q0, pl.multiple_of(q0 - LEFT_CHUNKS * CHUNK, tq), BAND_KEYS, 0)
        return 0

    n_groups = q_ref.shape[1] // tq - lead
    assert n_groups % 2 == 0
    lax.fori_loop(0, n_groups // 2, body, 0)


def _band_bias_body(rev_ref, o_ref):
    tq, n = ATT_TILE, rev_ref.shape[-1]
    rolled = pltpu.roll(jnp.broadcast_to(rev_ref[0], (tq, n)), 1, 1, stride=1, stride_axis=0)
    q_chunk = lax.broadcasted_iota(jnp.int32, (tq, BAND_KEYS), 0) // CHUNK
    k_chunk = lax.broadcasted_iota(jnp.int32, (tq, BAND_KEYS), 1) // CHUNK
    in_band = (k_chunk >= q_chunk) & (k_chunk <= q_chunk + LEFT_CHUNKS)
    o_ref[0] = jnp.where(in_band, rolled[:, tq:tq + BAND_KEYS] * LOG2E, -jnp.inf)


def _band_bias_table(rel_bias):
    tq, n_heads = ATT_TILE, rel_bias.shape[0]
    shift = LEFT_CHUNKS * CHUNK
    assert shift - (BAND_KEYS - 1) >= -REL_CLIP and shift + tq - 1 >= REL_CLIP
    n_rel = tq - 1 + BAND_KEYS
    n_clipped = n_rel - (tq - 1 + REL_CLIP + 1)
    by_rel = jnp.concatenate(
        [rel_bias[:, REL_CLIP - (tq - 1):], jnp.broadcast_to(rel_bias[:, -1:], (n_heads, n_clipped))], axis=1)
    rev = jnp.pad(by_rel[:, ::-1], ((0, 0), (0, 1)))
    assert rev.shape[1] == tq + BAND_KEYS and shift + tq - 1 == n_rel - tq
    return pl.pallas_call(
        _band_bias_body,
        grid=(n_heads,),
        in_specs=[pl.BlockSpec((1, 1, rev.shape[1]), lambda h: (h, 0, 0))],
        out_specs=pl.BlockSpec((1, tq, BAND_KEYS), lambda h: (h, 0, 0)),
        out_shape=jax.ShapeDtypeStruct((n_heads, tq, BAND_KEYS), F32),
        compiler_params=_params(1, 8 * tq * rev.shape[1] * 4),
        name="band_bias",
    )(rev.reshape(n_heads, 1, rev.shape[1]))


def _band_attention(q, k, v, bias_table):
    bsz, seq, _ = q.shape
    g = BAND_HEADS_PER_STEP
    tq = ATT_TILE
    width = g * C_DIM
    seq_spec = pl.BlockSpec((1, seq, width), lambda b, hg: (b, 0, hg))
    nbytes = 2 * (4 * seq * width * 2 + g * tq * BAND_KEYS * 4) + 4 * g * tq * BAND_KEYS * 4
    return pl.pallas_call(
        _band_body,
        grid=(bsz, C_HEADS // g),
        in_specs=[seq_spec, seq_spec, seq_spec,
                  pl.BlockSpec((g, tq, BAND_KEYS), lambda b, hg: (hg, 0, 0))],
        out_specs=seq_spec,
        out_shape=jax.ShapeDtypeStruct((bsz, seq, C_HEADS * C_DIM), BF16),
        compiler_params=_params(2, nbytes),
        name="band_attention",
    )(q, k, v, bias_table)


def _mix_ffn_body(*refs, n_mix, final_norm):
    h_ref, gffn_ref = refs[0], refs[1]
    mix_refs = refs[2:2 + n_mix]
    wo_ref, wg_ref, wu_ref, wd_ref, gfin_ref, o_ref, acc_ref = refs[2 + n_mix:]
    mixed = jnp.concatenate([r[...] for r in mix_refs], axis=1) if n_mix > 1 else mix_refs[0][...]
    h = h_ref[...] + _dot(mixed, wo_ref[...])
    u = _rms(h, gffn_ref[...]).astype(BF16)
    d_ff = wg_ref.shape[1]
    acc_ref[...] = h
    for f0 in range(0, d_ff, FFN_COL_TILE):
        cols = slice(f0, f0 + FFN_COL_TILE)
        gate = _dot(u, wg_ref[:, cols])
        act = (gate * jax.nn.sigmoid(gate) * _dot(u, wu_ref[:, cols])).astype(BF16)
        acc_ref[...] += _dot(act, wd_ref[cols, :])
    out = acc_ref[...]
    o_ref[...] = _rms(out, gfin_ref[...]) if final_norm else out


def _mix_ffn(h, g_ffn, mix_outs, w_out, wg, wu, wd, g_final, final_norm):
    t, d = h.shape
    tm = FFN_ROW_TILE
    d_ff = wg.shape[1]
    assert d_ff % FFN_COL_TILE == 0
    row = lambda n: pl.BlockSpec((tm, n), lambda i: (i, 0))
    nbytes = ((3 * d * d_ff + w_out.size) * 2 + 2 * tm * d * (4 + 4) + 2 * tm * w_out.shape[0] * 2
              + 3 * tm * d * 4 + 4 * tm * FFN_COL_TILE * 4)
    return pl.pallas_call(
        functools.partial(_mix_ffn_body, n_mix=len(mix_outs), final_norm=final_norm),
        grid=(t // tm,),
        in_specs=[row(d), _resident(g_ffn.shape)] + [row(o.shape[1]) for o in mix_outs]
                 + [_resident(w_out.shape), _resident(wg.shape), _resident(wu.shape), _resident(wd.shape),
                    _resident(g_final.shape)],
        out_specs=row(d),
        out_shape=jax.ShapeDtypeStruct((t, d), F32),
        scratch_shapes=[pltpu.VMEM((tm, d), F32)],
        compiler_params=_params(1, nbytes),
        name="mix_ffn",
    )(h, g_ffn, *mix_outs, w_out, wg, wu, wd, g_final)


def _cast_body(*refs):
    n = len(refs) // 2
    for src, dst in zip(refs[:n], refs[n:]):
        dst[...] = src[0].astype(BF16)


def _cast_weights(stacks, indices):
    steps = WEIGHT_CAST_STEPS
    blocks = [(w.shape[1] // steps, w.shape[2]) for w in stacks]
    assert all(w.shape[1] % (steps * 16) == 0 for w in stacks)
    nbytes = 2 * sum(r * c * 6 for r, c in blocks)
    return pl.pallas_call(
        _cast_body,
        grid=(steps,),
        in_specs=[pl.BlockSpec((1, r, c), lambda i, l=l: (l, i, 0)) for (r, c), l in zip(blocks, indices)],
        out_specs=[pl.BlockSpec((r, c), lambda i: (i, 0)) for r, c in blocks],
        out_shape=[jax.ShapeDtypeStruct(w.shape[1:], BF16) for w in stacks],
        compiler_params=_params(1, nbytes),
        name="cast_weights",
    )(*stacks)


def _head_slots(w, n_heads, per_head, start, keep):
    k = w.shape[0]
    w = w.reshape(k, n_heads, per_head)[:, :, start:start + keep]
    return jnp.pad(w, ((0, 0), (0, 0), (0, LANES - keep))).reshape(k, n_heads * LANES)


def _rope_slot_tables(seq, scale):
    half = MLA_ROPE // 2
    pos = np.arange(seq, dtype=np.float64)
    inv_freq = ROPE_THETA ** (-np.arange(0, MLA_ROPE, 2, dtype=np.float64) / MLA_ROPE)
    ang = pos[:, None] * inv_freq[None, :]
    cos, sin = np.cos(ang), np.sin(ang)
    zeros = np.zeros((seq, half))
    pad = np.zeros((seq, LANES - MLA_NOPE - MLA_ROPE))
    ones = np.ones((seq, MLA_NOPE))
    direct = np.concatenate([ones, cos, cos, pad], axis=1)
    from_lower = np.concatenate([0 * ones, zeros, sin, pad], axis=1)
    from_upper = np.concatenate([0 * ones, -sin, zeros, pad], axis=1)
    return jnp.asarray((np.stack([direct, from_lower, from_upper]) * scale).astype(np.float32))


def kernel(x, ev_w_in, ev_g_cq, ev_w_uq, ev_g_ckv, ev_w_ukv, ev_w_out, od_w_qkv, od_rel_bias, od_w_out,
           g_mix, g_ffn, w_gate, w_up, w_down, g_final):
    bsz, seq, d = x.shape
    depth = g_mix.shape[0]
    t = bsz * seq
    h = x.reshape(t, d)
    g_fin = g_final.reshape(1, d)
    rope_q = _rope_slot_tables(seq, (MLA_NOPE + MLA_ROPE) ** -0.5 * LOG2E)
    rope_k = _rope_slot_tables(seq, 1.0)

    for layer in range(depth):
        i = layer // 2
        gm = g_mix[layer].reshape(1, d)
        if layer % 2 == 0:
            w_in = ev_w_in[i]
            o1, o2, o3 = Q_LORA, Q_LORA + KV_LORA, Q_LORA + KV_LORA + MLA_ROPE
            wcq = w_in[:, :o1].astype(BF16)
            wckv = w_in[:, o1:o2].astype(BF16)
            wkr = jnp.pad(w_in[:, o2:o3], ((0, 0), (MLA_NOPE, LANES - MLA_NOPE - MLA_ROPE))).astype(BF16)
            wb = w_in[:, o3:].astype(BF16)
            wuq = _head_slots(ev_w_uq[i], MLA_HEADS, MLA_NOPE + MLA_ROPE, 0, MLA_NOPE + MLA_ROPE).astype(BF16)
            wuk = _head_slots(ev_w_ukv[i], MLA_HEADS, MLA_NOPE + MLA_V, 0, MLA_NOPE).astype(BF16)
            wuv = _head_slots(ev_w_ukv[i], MLA_HEADS, MLA_NOPE + MLA_V, MLA_NOPE, MLA_V).astype(BF16)
            qa, ka, va, qb, kb, vb = _even_proj(
                h, gm, wcq, wckv, wkr, wb, ev_g_cq[i].reshape(1, -1), wuq, ev_g_ckv[i].reshape(1, -1),
                wuk, wuv, rope_q, rope_k, seq)
            r3 = lambda a: a.reshape(bsz, seq, a.shape[1])
            o_a = _mla_attention(r3(qa), r3(ka), r3(va)).reshape(t, -1)
            o_b = _sb_attention(r3(qb), r3(kb), r3(vb)).reshape(t, -1)
            mix_outs, w_out_stack = [o_a, o_b], ev_w_out
        else:
            (w_qkv,) = _cast_weights([od_w_qkv], [i])
            q, k, v = _odd_proj(h, gm, w_qkv)
            r3 = lambda a: a.reshape(bsz, seq, a.shape[1])
            o_c = _band_attention(r3(q), r3(k), r3(v), _band_bias_table(od_rel_bias[i])).reshape(t, -1)
            mix_outs, w_out_stack = [o_c], od_w_out
        w_out, wg, wu, wd = _cast_weights([w_out_stack, w_gate, w_up, w_down], [i, layer, layer, layer])
        h = _mix_ffn(h, g_ffn[layer].reshape(1, d), mix_outs, w_out, wg, wu, wd, g_fin,
                     final_norm=(layer == depth - 1))
    return h.reshape(bsz, seq, d)
```

```python
import functools

import jax
import jax.numpy as jnp
import numpy as np
from jax import lax
from jax.experimental import pallas as pl
from jax.experimental.pallas import tpu as pltpu

F32 = jnp.float32
BF16 = jnp.bfloat16

CHUNK = 64
MLA_HEADS = 8
MLA_NOPE = 64
MLA_ROPE = 32
MLA_V = 64
Q_LORA = 384
KV_LORA = 256
ROPE_THETA = 10000.0
SB_HEADS = 8
SB_DIM = 64
C_HEADS = 16
C_DIM = 64
LEFT_CHUNKS = 8
REL_CLIP = 256
RMS_EPS = 1e-6

LANES = 128
V7X_VMEM_BYTES = 64 * 1024 * 1024

ROW_TILE = 512
FFN_ROW_TILE = 512
FFN_COL_TILE = 256
Q_TILE = 512
K_TILE = 256
HEADS_PER_STEP = 8
MLA_K_TILE = 512
MLA_DIAG_TILE = 256
MLA_HEADS_PER_STEP = 8
BAND_HEADS_PER_STEP = 8
ATT_TILE = 256
BAND_GROUP = ATT_TILE // CHUNK
BAND_KEYS = (LEFT_CHUNKS + BAND_GROUP) * CHUNK
WEIGHT_CAST_STEPS = 8

SB_HEAD_ROWS = 192
LOG2E = 1.4426950408889634
EXP_UNDERFLOW = -104.0


def _vmem_limit(nbytes):
    return int(min(max(2 * nbytes, 32 * 1024 * 1024), V7X_VMEM_BYTES - 8 * 1024 * 1024))


def _params(n_axes, nbytes):
    return pltpu.CompilerParams(
        dimension_semantics=("arbitrary",) * n_axes,
        vmem_limit_bytes=_vmem_limit(nbytes),
    )


def _dot(a, b):
    return jnp.dot(a, b, preferred_element_type=F32)


def _dot_nt(a, b):
    return lax.dot_general(a, b, (((1,), (1,)), ((), ())), preferred_element_type=F32)


def _rms(x, g):
    return x * lax.rsqrt(jnp.mean(x * x, axis=-1, keepdims=True) + RMS_EPS) * g


def _resident(shape):
    nd = len(shape)
    return pl.BlockSpec(shape, lambda *_: (0,) * nd, pipeline_mode=pl.Buffered(1))


def _rope(x, tab_ref):
    return (x * tab_ref[0]
            + pltpu.roll(x, MLA_ROPE // 2, 1) * tab_ref[1]
            + pltpu.roll(x, LANES - MLA_ROPE // 2, 1) * tab_ref[2])


def _even_proj_body(h_ref, gmix_ref, wcq_ref, wckv_ref, wkr_ref, wb_ref, gcq_ref, wuq_ref,
                    gckv_ref, wukv_ref, ropeq_ref, ropek_ref,
                    qa_ref, ka_ref, va_ref, qb_ref, kb_ref, vb_ref):
    u = _rms(h_ref[...], gmix_ref[...]).astype(BF16)
    nb = SB_HEADS * SB_DIM
    cq = _dot(u, wcq_ref[...])
    ckv = _dot(u, wckv_ref[...])
    qb_ref[...] = (_dot(u, wb_ref[:, :nb]) * (SB_DIM ** -0.5)).astype(BF16)
    cqn = _rms(cq, gcq_ref[...]).astype(BF16)
    ckvn = _rms(ckv, gckv_ref[...]).astype(BF16)
    k_rope = _rope(_dot(u, wkr_ref[...]), ropek_ref)
    qa = _dot(cqn, wuq_ref[...])
    kv = _dot(ckvn, wukv_ref[...])
    lane = lax.broadcasted_iota(jnp.int32, k_rope.shape, 1)
    v_tail = jnp.where(lane == MLA_V, 1.0, 0.0)
    for hh in range(MLA_HEADS):
        sl = slice(hh * LANES, (hh + 1) * LANES)
        qa_ref[:, sl] = _rope(qa[:, sl], ropeq_ref).astype(BF16)
        ka_ref[:, sl] = jnp.where(lane < MLA_NOPE, kv[:, sl], k_rope).astype(BF16)
        va_ref[:, sl] = jnp.where(lane < MLA_V, pltpu.roll(kv[:, sl], LANES - MLA_NOPE, 1), v_tail).astype(BF16)

    kvb = _dot(u, wb_ref[:, nb:])
    kb_ref[...] = kvb[:, :nb].astype(BF16)
    vb_ref[...] = kvb[:, nb:].astype(BF16)


def _even_proj(h, gmix, wcq, wckv, wkr, wb, gcq, wuq, gckv, wukv, ropeq, ropek, seq):
    t, d = h.shape
    tm = ROW_TILE
    per_seq = seq // tm
    assert MLA_NOPE + MLA_V == LANES and wukv.shape[1] == MLA_HEADS * LANES
    row = lambda n: pl.BlockSpec((tm, n), lambda i: (i, 0))
    rope_spec = pl.BlockSpec((3, tm, LANES), lambda i: (0, i % per_seq, 0))
    weights = (wcq, wckv, wkr, wb, gcq, wuq, gckv, wukv)
    out_widths = (MLA_HEADS * LANES,) * 3 + (SB_HEADS * SB_DIM,) * 3
    nbytes = (sum(w.size * w.dtype.itemsize for w in weights)
              + 2 * tm * d * 4 + 2 * tm * sum(out_widths) * 2 + tm * 8192 * 4)
    return pl.pallas_call(
        _even_proj_body,
        grid=(t // tm,),
        in_specs=[row(d), _resident(gmix.shape)]
                 + [_resident(wcq.shape), _resident(wckv.shape), _resident(wkr.shape), _resident(wb.shape),
                    _resident(gcq.shape), _resident(wuq.shape), _resident(gckv.shape), _resident(wukv.shape),
                    rope_spec, rope_spec],
        out_specs=[row(n) for n in out_widths],
        out_shape=[jax.ShapeDtypeStruct((t, n), BF16) for n in out_widths],
        compiler_params=_params(1, nbytes),
        name="even_proj",
    )(h, gmix, wcq, wckv, wkr, wb, gcq, wuq, gckv, wukv, ropeq, ropek)


def _head_select(shape, head_in_pair):
    lane = lax.broadcasted_iota(jnp.int32, shape, 1)
    return (lane < SB_DIM) if head_in_pair == 0 else (lane >= SB_DIM)


def _mla_body(q_ref, k_ref, v_ref, o_ref, m_ref, acc_ref):
    tq, tk, HEADS_PER_STEP = Q_TILE, MLA_K_TILE, MLA_HEADS_PER_STEP
    assert tq == tk
    dk = MLA_DIAG_TILE
    qi = pl.program_id(2)
    row_chunk = lax.broadcasted_iota(jnp.int32, (tq, dk), 0) // CHUNK
    col_chunk = lax.broadcasted_iota(jnp.int32, (tq, dk), 1) // CHUNK
    slots = [slice(hh * LANES, (hh + 1) * LANES) for hh in range(HEADS_PER_STEP)]
    qs = [q_ref[0, :, sl] for sl in slots]

    def tile(k0, n_keys, visible, first_row=0, first_visit=False):
        rows = pl.ds(pl.multiple_of(k0, n_keys), n_keys)
        heads = range(HEADS_PER_STEP)
        scores = [_dot_nt(qs[hh][first_row:], k_ref[0, rows, slots[hh]]) for hh in heads]
        probs = []
        for hh in heads:
            s = scores[hh] if visible is None else jnp.where(visible, scores[hh], -jnp.inf)
            m_new = jnp.max(s, axis=-1, keepdims=True)
            if first_visit:
                m_new = jnp.broadcast_to(m_new, (tq - first_row, LANES))
            else:
                m = m_ref[hh, first_row:, :]
                m_new = jnp.maximum(m, m_new)
                acc_ref[hh, first_row:, :] *= jnp.exp2(m - m_new)
            p = [jnp.exp2(s[:, j:j + LANES] - m_new) for j in range(0, n_keys, LANES)]
            probs.append(jnp.concatenate(p, axis=1).astype(BF16))
            m_ref[hh, first_row:, :] = m_new
        for hh in heads:
            pv = _dot(probs[hh], v_ref[0, rows, slots[hh]])
            if first_visit:
                acc_ref[hh, first_row:, :] = pv
            else:
                acc_ref[hh, first_row:, :] += pv

    for d in range(tq // dk):
        first = d * dk
        tile(qi * tq + first, dk, (col_chunk <= row_chunk)[:tq - first], first_row=first, first_visit=(d == 0))

    def full_tile(kb, _):
        tile(kb * tk, tk, None)
        return 0

    lax.fori_loop(0, qi, full_tile, 0)
    outs = [acc_ref[hh] / acc_ref[hh, :, MLA_V:MLA_V + 1] for hh in range(HEADS_PER_STEP)]
    lane = lax.broadcasted_iota(jnp.int32, (tq, LANES), 1)
    for p in range(HEADS_PER_STEP // 2):
        packed = jnp.where(lane < MLA_V, outs[2 * p], pltpu.roll(outs[2 * p + 1], MLA_V, 1))
        o_ref[0, :, p * LANES:(p + 1) * LANES] = packed.astype(BF16)


def _mla_attention(qa, ka, va):
    bsz, seq, _ = qa.shape
    tq, g = Q_TILE, MLA_HEADS_PER_STEP
    nbytes = 2 * (2 * seq * g * LANES * 2 + tq * g * LANES * 3) + 8 * g * tq * MLA_K_TILE * 4
    return pl.pallas_call(
        _mla_body,
        grid=(bsz, MLA_HEADS // g, seq // tq),
        in_specs=[pl.BlockSpec((1, tq, g * LANES), lambda b, hg, qi: (b, qi, hg)),
                  pl.BlockSpec((1, seq, g * LANES), lambda b, hg, qi: (b, 0, hg)),
                  pl.BlockSpec((1, seq, g * LANES), lambda b, hg, qi: (b, 0, hg))],
        out_specs=pl.BlockSpec((1, tq, g * MLA_V), lambda b, hg, qi: (b, qi, hg)),
        out_shape=jax.ShapeDtypeStruct((bsz, seq, MLA_HEADS * MLA_V), BF16),
        scratch_shapes=[pltpu.VMEM((g, tq, LANES), F32), pltpu.VMEM((g, tq, LANES), F32)],
        compiler_params=_params(3, nbytes),
        name="mla_attention",
    )(qa, ka, va)


def _sb_body(q_ref, k_ref, v_ref, ntri_ref, o_ref, later_ref, acc_ref):
    tq, tk = Q_TILE, K_TILE
    qi = pl.program_id(2)
    per = tq // tk
    row = lax.broadcasted_iota(jnp.int32, (tq, tk), 0)
    col = lax.broadcasted_iota(jnp.int32, (tq, tk), 1)
    pair_slots = [slice(p * LANES, (p + 1) * LANES) for p in range(HEADS_PER_STEP // 2)]
    qs = []
    for hh in range(HEADS_PER_STEP):
        q_pair = q_ref[0, :, pair_slots[hh // 2]]
        qs.append(jnp.where(_head_select((tq, LANES), hh % 2), q_pair, jnp.zeros_like(q_pair)))

    def tile(kb, before, first_row=0, last_row=Q_TILE):
        rows = pl.ds(pl.multiple_of(kb * tk, tk), tk)
        heads = range(HEADS_PER_STEP)
        zs = [_dot_nt(qs[hh][first_row:last_row], k_ref[0, rows, pair_slots[hh // 2]]) for hh in heads]
        softplus = []
        for hh in heads:
            sp = jnp.maximum(zs[hh], 0.0) + jnp.log(1.0 + jnp.exp2(jnp.abs(zs[hh]) * -LOG2E))
            softplus.append(sp if before is None else jnp.where(before, sp, 0.0))
        cums = []
        for hh in heads:
            local = _dot(softplus[hh].astype(BF16), ntri_ref[...])
            later = later_ref[hh, first_row:last_row, :]
            cums.append(jnp.concatenate([local[:, j:j + LANES] + later for j in range(0, tk, LANES)], axis=1))
        weights, largest = [], None
        for hh in heads:
            w = jnp.exp((zs[hh] - softplus[hh]) + cums[hh])
            if before is not None:
                w = jnp.where(before, w, 0.0)
            weights.append(w.astype(BF16))
            later = cums[hh][:, 0:1] - softplus[hh][:, 0:1]
            later_ref[hh, first_row:last_row, :] = jnp.broadcast_to(later, (last_row - first_row, LANES))
            largest = later if largest is None else jnp.maximum(largest, later)
        for hh in heads:
            acc_ref[hh, first_row:last_row, :] += _dot(weights[hh], v_ref[0, rows, pair_slots[hh // 2]])
        if first_row:
            return None
        rest = jnp.max(largest[SB_HEAD_ROWS:]) if last_row > SB_HEAD_ROWS else None
        return jnp.max(largest[:SB_HEAD_ROWS]), rest

    later_ref[...] = jnp.zeros(later_ref.shape, F32)
    acc_ref[...] = jnp.zeros(acc_ref.shape, F32)
    for d in reversed(range(per)):
        first = d * tk
        largest_sums = tile(per * qi + d, (col < row)[:tq - first], first_row=first)

    n_earlier = per * qi

    def earlier_tile(state):
        i, head_sum, rest_sum = state
        kb = n_earlier - 1 - i
        head_sum, rest_sum = lax.cond(
            rest_sum <= EXP_UNDERFLOW,
            lambda: (tile(kb, None, last_row=SB_HEAD_ROWS)[0], rest_sum),
            lambda: tile(kb, None))
        return i + 1, head_sum, rest_sum

    lax.while_loop(
        lambda state: jnp.logical_and(state[0] < n_earlier, jnp.maximum(state[1], state[2]) > EXP_UNDERFLOW),
        earlier_tile, (jnp.int32(0),) + largest_sums)
    for p in range(HEADS_PER_STEP // 2):
        packed = jnp.where(_head_select((tq, LANES), 0), acc_ref[2 * p], acc_ref[2 * p + 1])
        o_ref[0, :, pair_slots[p]] = packed.astype(BF16)


def _sb_attention(qb, kb, vb):
    bsz, seq, _ = qb.shape
    tq, tk, g = Q_TILE, K_TILE, HEADS_PER_STEP
    width = g * SB_DIM
    j = np.arange(tk)
    ntri2 = jnp.asarray(np.where(j[:, None] > j[None, :], -1.0, 0.0), dtype=BF16)
    nbytes = 2 * (2 * seq * width * 2 + 2 * tq * width * 2) + tk * tk * 2 + 8 * g * tq * tk * 4
    return pl.pallas_call(
        _sb_body,
        grid=(bsz, SB_HEADS // g, seq // tq),
        in_specs=[pl.BlockSpec((1, tq, width), lambda b, hg, qi: (b, qi, hg)),
                  pl.BlockSpec((1, seq, width), lambda b, hg, qi: (b, 0, hg)),
                  pl.BlockSpec((1, seq, width), lambda b, hg, qi: (b, 0, hg)),
                  _resident(ntri2.shape)],
        out_specs=pl.BlockSpec((1, tq, width), lambda b, hg, qi: (b, qi, hg)),
        out_shape=jax.ShapeDtypeStruct((bsz, seq, SB_HEADS * SB_DIM), BF16),
        scratch_shapes=[pltpu.VMEM((g, tq, LANES), F32), pltpu.VMEM((g, tq, LANES), F32)],
        compiler_params=_params(3, nbytes),
        name="sb_attention",
    )(qb, kb, vb, ntri2)


def _odd_proj_body(h_ref, gmix_ref, w_ref, q_ref, k_ref, v_ref):
    u = _rms(h_ref[...], gmix_ref[...]).astype(BF16)
    n = C_HEADS * C_DIM
    qkv = _dot(u, w_ref[...])
    q_ref[...] = (qkv[:, :n] * (C_DIM ** -0.5 * LOG2E)).astype(BF16)
    k_ref[...] = qkv[:, n:2 * n].astype(BF16)
    v_ref[...] = qkv[:, 2 * n:].astype(BF16)


def _odd_proj(h, gmix, w_qkv):
    t, d = h.shape
    tm = ROW_TILE
    n = C_HEADS * C_DIM
    row = lambda m: pl.BlockSpec((tm, m), lambda i: (i, 0))
    nbytes = w_qkv.size * 2 + 2 * tm * d * 4 + 2 * 3 * tm * n * 2 + 2 * tm * 3 * n * 4
    return pl.pallas_call(
        _odd_proj_body,
        grid=(t // tm,),
        in_specs=[row(d), _resident(gmix.shape), _resident(w_qkv.shape)],
        out_specs=[row(n)] * 3,
        out_shape=[jax.ShapeDtypeStruct((t, n), BF16)] * 3,
        compiler_params=_params(1, nbytes),
        name="odd_proj",
    )(h, gmix, w_qkv)


def _band_body(q_ref, k_ref, v_ref, bias_ref, o_ref):
    tq = ATT_TILE

    def group(q0, k0, n_keys, bias_off):
        heads = range(BAND_HEADS_PER_STEP)
        pair_slots = [slice(p * LANES, (p + 1) * LANES) for p in range(BAND_HEADS_PER_STEP // 2)]
        scores = []
        for hh in heads:
            q_pair = q_ref[0, pl.ds(q0, tq), pair_slots[hh // 2]]
            q = jnp.where(_head_select((tq, LANES), hh % 2), q_pair, jnp.zeros_like(q_pair))
            scores.append(_dot_nt(q, k_ref[0, pl.ds(k0, n_keys), pair_slots[hh // 2]]))
        probs = []
        for hh in heads:
            s = scores[hh] + bias_ref[hh, :, bias_off:bias_off + n_keys]
            probs.append(jnp.exp2(s - jnp.max(s, axis=-1, keepdims=True)).astype(BF16))
        outs = []
        for hh in heads:
            v_pair = v_ref[0, pl.ds(k0, n_keys), pair_slots[hh // 2]]
            vals = jnp.where(_head_select((n_keys, LANES), hh % 2), v_pair, jnp.ones_like(v_pair))
            pv = _dot(probs[hh], vals)
            row_sum_lane = (1 - hh % 2) * C_DIM
            outs.append(pv / pv[:, row_sum_lane:row_sum_lane + 1])
        for p in range(BAND_HEADS_PER_STEP // 2):
            packed = jnp.where(_head_select((tq, LANES), 0), outs[2 * p], outs[2 * p + 1])
            o_ref[0, pl.ds(q0, tq), pair_slots[p]] = packed.astype(BF16)

    lead = LEFT_CHUNKS // BAND_GROUP
    for g in range(lead):
        group(g * tq, 0, (g + 1) * tq, LEFT_CHUNKS * CHUNK - g * tq)

    def body(i, _):
        for half in range(2):
            q0 = pl.multiple_of((lead + 2 * i + half) * tq, tq)
            group(q0, pl.multiple_of(q0 - LEFT_CHUNKS * CHUNK, tq), BAND_KEYS, 0)
        return 0

    n_groups = q_ref.shape[1] // tq - lead
    assert n_groups % 2 == 0
    lax.fori_loop(0, n_groups // 2, body, 0)


def _band_bias_body(rev_ref, o_ref):
    tq, n = ATT_TILE, rev_ref.shape[-1]
    rolled = pltpu.roll(jnp.broadcast_to(rev_ref[0], (tq, n)), 1, 1, stride=1, stride_axis=0)
    q_chunk = lax.broadcasted_iota(jnp.int32, (tq, BAND_KEYS), 0) // CHUNK
    k_chunk = lax.broadcasted_iota(jnp.int32, (tq, BAND_KEYS), 1) // CHUNK
    in_band = (k_chunk >= q_chunk) & (k_chunk <= q_chunk + LEFT_CHUNKS)
    o_ref[0] = jnp.where(in_band, rolled[:, tq:tq + BAND_KEYS] * LOG2E, -jnp.inf)


def _band_bias_table(rel_bias):
    tq, n_heads = ATT_TILE, rel_bias.shape[0]
    shift = LEFT_CHUNKS * CHUNK
    assert shift - (BAND_KEYS - 1) >= -REL_CLIP and shift + tq - 1 >= REL_CLIP
    n_rel = tq - 1 + BAND_KEYS
    n_clipped = n_rel - (tq - 1 + REL_CLIP + 1)
    by_rel = jnp.concatenate(
        [rel_bias[:, REL_CLIP - (tq - 1):], jnp.broadcast_to(rel_bias[:, -1:], (n_heads, n_clipped))], axis=1)
    rev = jnp.pad(by_rel[:, ::-1], ((0, 0), (0, 1)))
    assert rev.shape[1] == tq + BAND_KEYS and shift + tq - 1 == n_rel - tq
    return pl.pallas_call(
        _band_bias_body,
        grid=(n_heads,),
        in_specs=[pl.BlockSpec((1, 1, rev.shape[1]), lambda h: (h, 0, 0))],
        out_specs=pl.BlockSpec((1, tq, BAND_KEYS), lambda h: (h, 0, 0)),
        out_shape=jax.ShapeDtypeStruct((n_heads, tq, BAND_KEYS), F32),
        compiler_params=_params(1, 8 * tq * rev.shape[1] * 4),
        name="band_bias",
    )(rev.reshape(n_heads, 1, rev.shape[1]))


def _band_attention(q, k, v, bias_table):
    bsz, seq, _ = q.shape
    g = BAND_HEADS_PER_STEP
    tq = ATT_TILE
    width = g * C_DIM
    seq_spec = pl.BlockSpec((1, seq, width), lambda b, hg: (b, 0, hg))
    nbytes = 2 * (4 * seq * width * 2 + g * tq * BAND_KEYS * 4) + 4 * g * tq * BAND_KEYS * 4
    return pl.pallas_call(
        _band_body,
        grid=(bsz, C_HEADS // g),
        in_specs=[seq_spec, seq_spec, seq_spec,
                  pl.BlockSpec((g, tq, BAND_KEYS), lambda b, hg: (hg, 0, 0))],
        out_specs=seq_spec,
        out_shape=jax.ShapeDtypeStruct((bsz, seq, C_HEADS * C_DIM), BF16),
        compiler_params=_params(2, nbytes),
        name="band_attention",
    )(q, k, v, bias_table)


def _mix_ffn_body(*refs, n_mix, final_norm):
    h_ref, gffn_ref = refs[0], refs[1]
    mix_refs = refs[2:2 + n_mix]
    wo_ref, wg_ref, wu_ref, wd_ref, gfin_ref, o_ref, acc_ref = refs[2 + n_mix:]
    mixed = jnp.concatenate([r[...] for r in mix_refs], axis=1) if n_mix > 1 else mix_refs[0][...]
    h = h_ref[...] + _dot(mixed, wo_ref[...])
    u = _rms(h, gffn_ref[...]).astype(BF16)
    d_ff = wg_ref.shape[1]
    acc_ref[...] = h
    for f0 in range(0, d_ff, FFN_COL_TILE):
        cols = slice(f0, f0 + FFN_COL_TILE)
        gate = _dot(u, wg_ref[:, cols])
        act = (gate * jax.nn.sigmoid(gate) * _dot(u, wu_ref[:, cols])).astype(BF16)
        acc_ref[...] += _dot(act, wd_ref[cols, :])
    out = acc_ref[...]
    o_ref[...] = _rms(out, gfin_ref[...]) if final_norm else out


def _mix_ffn(h, g_ffn, mix_outs, w_out, wg, wu, wd, g_final, final_norm):
    t, d = h.shape
    tm = FFN_ROW_TILE
    d_ff = wg.shape[1]
    assert d_ff % FFN_COL_TILE == 0
    row = lambda n: pl.BlockSpec((tm, n), lambda i: (i, 0))
    nbytes = ((3 * d * d_ff + w_out.size) * 2 + 2 * tm * d * (4 + 4) + 2 * tm * w_out.shape[0] * 2
              + 3 * tm * d * 4 + 4 * tm * FFN_COL_TILE * 4)
    return pl.pallas_call(
        functools.partial(_mix_ffn_body, n_mix=len(mix_outs), final_norm=final_norm),
        grid=(t // tm,),
        in_specs=[row(d), _resident(g_ffn.shape)] + [row(o.shape[1]) for o in mix_outs]
                 + [_resident(w_out.shape), _resident(wg.shape), _resident(wu.shape), _resident(wd.shape),
                    _resident(g_final.shape)],
        out_specs=row(d),
        out_shape=jax.ShapeDtypeStruct((t, d), F32),
        scratch_shapes=[pltpu.VMEM((tm, d), F32)],
        compiler_params=_params(1, nbytes),
        name="mix_ffn",
    )(h, g_ffn, *mix_outs, w_out, wg, wu, wd, g_final)


def _cast_body(*refs):
    n = len(refs) // 2
    for src, dst in zip(refs[:n], refs[n:]):
        dst[...] = src[0].astype(BF16)


def _cast_weights(stacks, indices):
    steps = WEIGHT_CAST_STEPS
    blocks = [(w.shape[1] // steps, w.shape[2]) for w in stacks]
    assert all(w.shape[1] % (steps * 16) == 0 for w in stacks)
    nbytes = 2 * sum(r * c * 6 for r, c in blocks)
    return pl.pallas_call(
        _cast_body,
        grid=(steps,),
        in_specs=[pl.BlockSpec((1, r, c), lambda i, l=l: (l, i, 0)) for (r, c), l in zip(blocks, indices)],
        out_specs=[pl.BlockSpec((r, c), lambda i: (i, 0)) for r, c in blocks],
        out_shape=[jax.ShapeDtypeStruct(w.shape[1:], BF16) for w in stacks],
        compiler_params=_params(1, nbytes),
        name="cast_weights",
    )(*stacks)


def _head_slots(w, n_heads, per_head, start, keep):
    k = w.shape[0]
    w = w.reshape(k, n_heads, per_head)[:, :, start:start + keep]
    return jnp.pad(w, ((0, 0), (0, 0), (0, LANES - keep))).reshape(k, n_heads * LANES)


def _rope_slot_tables(seq, scale):
    half = MLA_ROPE // 2
    pos = np.arange(seq, dtype=np.float64)
    inv_freq = ROPE_THETA ** (-np.arange(0, MLA_ROPE, 2, dtype=np.float64) / MLA_ROPE)
    ang = pos[:, None] * inv_freq[None, :]
    cos, sin = np.cos(ang), np.sin(ang)
    zeros = np.zeros((seq, half))
    pad = np.zeros((seq, LANES - MLA_NOPE - MLA_ROPE))
    ones = np.ones((seq, MLA_NOPE))
    direct = np.concatenate([ones, cos, cos, pad], axis=1)
    from_lower = np.concatenate([0 * ones, zeros, sin, pad], axis=1)
    from_upper = np.concatenate([0 * ones, -sin, zeros, pad], axis=1)
    return jnp.asarray((np.stack([direct, from_lower, from_upper]) * scale).astype(np.float32))


def kernel(x, ev_w_in, ev_g_cq, ev_w_uq, ev_g_ckv, ev_w_ukv, ev_w_out, od_w_qkv, od_rel_bias, od_w_out,
           g_mix, g_ffn, w_gate, w_up, w_down, g_final):
    bsz, seq, d = x.shape
    depth = g_mix.shape[0]
    t = bsz * seq
    h = x.reshape(t, d)
    g_fin = g_final.reshape(1, d)
    rope_q = _rope_slot_tables(seq, (MLA_NOPE + MLA_ROPE) ** -0.5 * LOG2E)
    rope_k = _rope_slot_tables(seq, 1.0)

    for layer in range(depth):
        i = layer // 2
        gm = g_mix[layer].reshape(1, d)
        if layer % 2 == 0:
            w_in = ev_w_in[i]
            o1, o2, o3 = Q_LORA, Q_LORA + KV_LORA, Q_LORA + KV_LORA + MLA_ROPE
            wcq = w_in[:, :o1].astype(BF16)
            wckv = w_in[:, o1:o2].astype(BF16)
            wkr = jnp.pad(w_in[:, o2:o3], ((0, 0), (MLA_NOPE, LANES - MLA_NOPE - MLA_ROPE))).astype(BF16)
            wb = w_in[:, o3:].astype(BF16)
            wuq = _head_slots(ev_w_uq[i], MLA_HEADS, MLA_NOPE + MLA_ROPE, 0, MLA_NOPE + MLA_ROPE).astype(BF16)
            qa, ka, va, qb, kb, vb = _even_proj(
                h, gm, wcq, wckv, wkr, wb, ev_g_cq[i].reshape(1, -1), wuq, ev_g_ckv[i].reshape(1, -1),
                ev_w_ukv[i].astype(BF16), rope_q, rope_k, seq)
            r3 = lambda a: a.reshape(bsz, seq, a.shape[1])
            o_a = _mla_attention(r3(qa), r3(ka), r3(va)).reshape(t, -1)
            o_b = _sb_attention(r3(qb), r3(kb), r3(vb)).reshape(t, -1)
            mix_outs, w_out_stack = [o_a, o_b], ev_w_out
        else:
            (w_qkv,) = _cast_weights([od_w_qkv], [i])
            q, k, v = _odd_proj(h, gm, w_qkv)
            r3 = lambda a: a.reshape(bsz, seq, a.shape[1])
            o_c = _band_attention(r3(q), r3(k), r3(v), _band_bias_table(od_rel_bias[i])).reshape(t, -1)
            mix_outs, w_out_stack = [o_c], od_w_out
        w_out, wg, wu, wd = _cast_weights([w_out_stack, w_gate, w_up, w_down], [i, layer, layer, layer])
        h = _mix_ffn(h, g_ffn[layer].reshape(1, d), mix_outs, w_out, wg, wu, wd, g_fin,
                     final_norm=(layer == depth - 1))
    return h.reshape(bsz, seq, d)
```

```python
import functools

import jax
import jax.numpy as jnp
import numpy as np
from jax import lax
from jax.experimental import pallas as pl
from jax.experimental.pallas import tpu as pltpu

F32 = jnp.float32
BF16 = jnp.bfloat16

CHUNK = 64
MLA_HEADS = 8
MLA_NOPE = 64
MLA_ROPE = 32
MLA_V = 64
Q_LORA = 384
KV_LORA = 256
ROPE_THETA = 10000.0
SB_HEADS = 8
SB_DIM = 64
C_HEADS = 16
C_DIM = 64
LEFT_CHUNKS = 8
REL_CLIP = 256
RMS_EPS = 1e-6

LANES = 128
V7X_VMEM_BYTES = 64 * 1024 * 1024

ROW_TILE = 512
FFN_ROW_TILE = 1024
FFN_COL_TILE = 256
Q_TILE = 512
K_TILE = 256
SB_HEADS_PER_STEP = 8
SB_HEAD_ROWS = 192
MLA_K_TILE = 512
MLA_DIAG_TILE = 256
MLA_HEADS_PER_STEP = 8
BAND_HEADS_PER_STEP = 8
ATT_TILE = 256
BAND_GROUP = ATT_TILE // CHUNK
BAND_KEYS = (LEFT_CHUNKS + BAND_GROUP) * CHUNK
WEIGHT_CAST_STEPS = 8

LOG2E = 1.4426950408889634
EXP_UNDERFLOW = -104.0


def _vmem_limit(nbytes):
    return int(min(max(2 * nbytes, 32 * 1024 * 1024), V7X_VMEM_BYTES - 8 * 1024 * 1024))


def _params(n_axes, nbytes):
    return pltpu.CompilerParams(
        dimension_semantics=("arbitrary",) * n_axes,
        vmem_limit_bytes=_vmem_limit(nbytes),
    )


def _dot(a, b):
    return jnp.dot(a, b, preferred_element_type=F32)


def _dot_nt(a, b):
    return lax.dot_general(a, b, (((1,), (1,)), ((), ())), preferred_element_type=F32)


def _rms(x, g):
    return x * lax.rsqrt(jnp.mean(x * x, axis=-1, keepdims=True) + RMS_EPS) * g


def _resident(shape):
    nd = len(shape)
    return pl.BlockSpec(shape, lambda *_: (0,) * nd, pipeline_mode=pl.Buffered(1))


def _rope(x, tab_ref):
    return (x * tab_ref[0]
            + pltpu.roll(x, MLA_ROPE // 2, 1) * tab_ref[1]
            + pltpu.roll(x, LANES - MLA_ROPE // 2, 1) * tab_ref[2])


def _even_proj_body(h_ref, gmix_ref, wcq_ref, wckv_ref, wkr_ref, wb_ref, gcq_ref, wuq_ref,
                    gckv_ref, wukv_ref, ropeq_ref, ropek_ref,
                    qa_ref, ka_ref, va_ref, qb_ref, kb_ref, vb_ref):
    u = _rms(h_ref[...], gmix_ref[...]).astype(BF16)
    nb = SB_HEADS * SB_DIM
    cq = _dot(u, wcq_ref[...])
    ckv = _dot(u, wckv_ref[...])
    qb_ref[...] = (_dot(u, wb_ref[:, :nb]) * (SB_DIM ** -0.5)).astype(BF16)
    cqn = _rms(cq, gcq_ref[...]).astype(BF16)
    ckvn = _rms(ckv, gckv_ref[...]).astype(BF16)
    k_rope = _rope(_dot(u, wkr_ref[...]), ropek_ref)
    qa = _dot(cqn, wuq_ref[...])
    kv = _dot(ckvn, wukv_ref[...])
    lane = lax.broadcasted_iota(jnp.int32, k_rope.shape, 1)
    v_tail = jnp.where(lane == MLA_V, 1.0, 0.0)
    for hh in range(MLA_HEADS):
        sl = slice(hh * LANES, (hh + 1) * LANES)
        qa_ref[:, sl] = _rope(qa[:, sl], ropeq_ref).astype(BF16)
        ka_ref[:, sl] = jnp.where(lane < MLA_NOPE, kv[:, sl], k_rope).astype(BF16)
        va_ref[:, sl] = jnp.where(lane < MLA_V, pltpu.roll(kv[:, sl], LANES - MLA_NOPE, 1), v_tail).astype(BF16)

    kvb = _dot(u, wb_ref[:, nb:])
    kb_ref[...] = kvb[:, :nb].astype(BF16)
    vb_ref[...] = kvb[:, nb:].astype(BF16)


def _even_proj(h, gmix, wcq, wckv, wkr, wb, gcq, wuq, gckv, wukv, ropeq, ropek, seq):
    t, d = h.shape
    tm = ROW_TILE
    per_seq = seq // tm
    assert MLA_NOPE + MLA_V == LANES and wukv.shape[1] == MLA_HEADS * LANES
    row = lambda n: pl.BlockSpec((tm, n), lambda i: (i, 0))
    rope_spec = pl.BlockSpec((3, tm, LANES), lambda i: (0, i % per_seq, 0))
    weights = (wcq, wckv, wkr, wb, gcq, wuq, gckv, wukv)
    out_widths = (MLA_HEADS * LANES,) * 3 + (SB_HEADS * SB_DIM,) * 3
    nbytes = (sum(w.size * w.dtype.itemsize for w in weights)
              + 2 * tm * d * 4 + 2 * tm * sum(out_widths) * 2 + tm * 8192 * 4)
    return pl.pallas_call(
        _even_proj_body,
        grid=(t // tm,),
        in_specs=[row(d), _resident(gmix.shape)]
                 + [_resident(wcq.shape), _resident(wckv.shape), _resident(wkr.shape), _resident(wb.shape),
                    _resident(gcq.shape), _resident(wuq.shape), _resident(gckv.shape), _resident(wukv.shape),
                    rope_spec, rope_spec],
        out_specs=[row(n) for n in out_widths],
        out_shape=[jax.ShapeDtypeStruct((t, n), BF16) for n in out_widths],
        compiler_params=_params(1, nbytes),
        name="even_proj",
    )(h, gmix, wcq, wckv, wkr, wb, gcq, wuq, gckv, wukv, ropeq, ropek)


def _head_select(shape, head_in_pair):
    lane = lax.broadcasted_iota(jnp.int32, shape, 1)
    return (lane < SB_DIM) if head_in_pair == 0 else (lane >= SB_DIM)


def _mla_body(q_ref, k_ref, v_ref, o_ref, m_ref, acc_ref):
    tq, tk, n_heads = Q_TILE, MLA_K_TILE, MLA_HEADS_PER_STEP
    assert tq == tk
    dk = MLA_DIAG_TILE
    qi = pl.program_id(2)
    row_chunk = lax.broadcasted_iota(jnp.int32, (tq, dk), 0) // CHUNK
    col_chunk = lax.broadcasted_iota(jnp.int32, (tq, dk), 1) // CHUNK
    slots = [slice(hh * LANES, (hh + 1) * LANES) for hh in range(n_heads)]
    qs = [q_ref[0, :, sl] for sl in slots]

    def tile(k0, n_keys, visible, first_row=0, first_visit=False):
        rows = pl.ds(pl.multiple_of(k0, n_keys), n_keys)
        heads = range(n_heads)
        scores = [_dot_nt(qs[hh][first_row:], k_ref[0, rows, slots[hh]]) for hh in heads]
        probs = []
        for hh in heads:
            s = scores[hh] if visible is None else jnp.where(visible, scores[hh], -jnp.inf)
            m_new = jnp.max(s, axis=-1, keepdims=True)
            if first_visit:
                m_new = jnp.broadcast_to(m_new, (tq - first_row, LANES))
            else:
                m = m_ref[hh, first_row:, :]
                m_new = jnp.maximum(m, m_new)
                acc_ref[hh, first_row:, :] *= jnp.exp2(m - m_new)
            p = [jnp.exp2(s[:, j:j + LANES] - m_new) for j in range(0, n_keys, LANES)]
            probs.append(jnp.concatenate(p, axis=1).astype(BF16))
            m_ref[hh, first_row:, :] = m_new
        for hh in heads:
            pv = _dot(probs[hh], v_ref[0, rows, slots[hh]])
            if first_visit:
                acc_ref[hh, first_row:, :] = pv
            else:
                acc_ref[hh, first_row:, :] += pv

    for d in range(tq // dk):
        first = d * dk
        tile(qi * tq + first, dk, (col_chunk <= row_chunk)[:tq - first], first_row=first, first_visit=(d == 0))

    def full_tile(kb, _):
        tile(kb * tk, tk, None)
        return 0

    lax.fori_loop(0, qi, full_tile, 0)
    outs = [acc_ref[hh] / acc_ref[hh, :, MLA_V:MLA_V + 1] for hh in range(n_heads)]
    lane = lax.broadcasted_iota(jnp.int32, (tq, LANES), 1)
    for p in range(n_heads // 2):
        packed = jnp.where(lane < MLA_V, outs[2 * p], pltpu.roll(outs[2 * p + 1], MLA_V, 1))
        o_ref[0, :, p * LANES:(p + 1) * LANES] = packed.astype(BF16)


def _mla_attention(qa, ka, va):
    bsz, seq, _ = qa.shape
    tq, g = Q_TILE, MLA_HEADS_PER_STEP
    nbytes = 2 * (2 * seq * g * LANES * 2 + tq * g * LANES * 3) + 8 * g * tq * MLA_K_TILE * 4
    return pl.pallas_call(
        _mla_body,
        grid=(bsz, MLA_HEADS // g, seq // tq),
        in_specs=[pl.BlockSpec((1, tq, g * LANES), lambda b, hg, qi: (b, qi, hg)),
                  pl.BlockSpec((1, seq, g * LANES), lambda b, hg, qi: (b, 0, hg)),
                  pl.BlockSpec((1, seq, g * LANES), lambda b, hg, qi: (b, 0, hg))],
        out_specs=pl.BlockSpec((1, tq, g * MLA_V), lambda b, hg, qi: (b, qi, hg)),
        out_shape=jax.ShapeDtypeStruct((bsz, seq, MLA_HEADS * MLA_V), BF16),
        scratch_shapes=[pltpu.VMEM((g, tq, LANES), F32), pltpu.VMEM((g, tq, LANES), F32)],
        compiler_params=_params(3, nbytes),
        name="mla_attention",
    )(qa, ka, va)


def _sb_body(q_ref, k_ref, v_ref, ntri_ref, o_ref, later_ref, acc_ref):
    tq, tk, n_heads = Q_TILE, K_TILE, SB_HEADS_PER_STEP
    qi = pl.program_id(2)
    per = tq // tk
    row = lax.broadcasted_iota(jnp.int32, (tq, tk), 0)
    col = lax.broadcasted_iota(jnp.int32, (tq, tk), 1)
    pair_slots = [slice(p * LANES, (p + 1) * LANES) for p in range(n_heads // 2)]
    qs = []
    for hh in range(n_heads):
        q_pair = q_ref[0, :, pair_slots[hh // 2]]
        qs.append(jnp.where(_head_select((tq, LANES), hh % 2), q_pair, jnp.zeros_like(q_pair)))

    def tile(kb, before, first_row=0, last_row=Q_TILE):
        rows = pl.ds(pl.multiple_of(kb * tk, tk), tk)
        heads = range(n_heads)
        zs = [_dot_nt(qs[hh][first_row:last_row], k_ref[0, rows, pair_slots[hh // 2]]) for hh in heads]
        softplus = []
        for hh in heads:
            sp = jnp.maximum(zs[hh], 0.0) + jnp.log(1.0 + jnp.exp2(jnp.abs(zs[hh]) * -LOG2E))
            softplus.append(sp if before is None else jnp.where(before, sp, 0.0))
        cums = []
        for hh in heads:
            local = _dot(softplus[hh].astype(BF16), ntri_ref[...])
            later = later_ref[hh, first_row:last_row, :]
            cums.append(jnp.concatenate([local[:, j:j + LANES] + later for j in range(0, tk, LANES)], axis=1))
        weights, largest = [], None
        for hh in heads:
            w = jnp.exp((zs[hh] - softplus[hh]) + cums[hh])
            if before is not None:
                w = jnp.where(before, w, 0.0)
            weights.append(w.astype(BF16))
            later = cums[hh][:, 0:1] - softplus[hh][:, 0:1]
            later_ref[hh, first_row:last_row, :] = jnp.broadcast_to(later, (last_row - first_row, LANES))
            largest = later if largest is None else jnp.maximum(largest, later)
        for hh in heads:
            acc_ref[hh, first_row:last_row, :] += _dot(weights[hh], v_ref[0, rows, pair_slots[hh // 2]])
        if first_row:
            return None
        rest = jnp.max(largest[SB_HEAD_ROWS:]) if last_row > SB_HEAD_ROWS else None
        return jnp.max(largest[:SB_HEAD_ROWS]), rest

    later_ref[...] = jnp.zeros(later_ref.shape, F32)
    acc_ref[...] = jnp.zeros(acc_ref.shape, F32)
    for d in reversed(range(per)):
        first = d * tk
        largest_sums = tile(per * qi + d, (col < row)[:tq - first], first_row=first)

    n_earlier = per * qi

    def earlier_tile(state):
        i, head_sum, rest_sum = state
        kb = n_earlier - 1 - i
        head_sum, rest_sum = lax.cond(
            rest_sum <= EXP_UNDERFLOW,
            lambda: (tile(kb, None, last_row=SB_HEAD_ROWS)[0], rest_sum),
            lambda: tile(kb, None))
        return i + 1, head_sum, rest_sum

    lax.while_loop(
        lambda state: jnp.logical_and(state[0] < n_earlier, jnp.maximum(state[1], state[2]) > EXP_UNDERFLOW),
        earlier_tile, (jnp.int32(0),) + largest_sums)
    for p in range(n_heads // 2):
        packed = jnp.where(_head_select((tq, LANES), 0), acc_ref[2 * p], acc_ref[2 * p + 1])
        o_ref[0, :, pair_slots[p]] = packed.astype(BF16)


def _sb_attention(qb, kb, vb):
    bsz, seq, _ = qb.shape
    tq, tk, g = Q_TILE, K_TILE, SB_HEADS_PER_STEP
    width = g * SB_DIM
    j = np.arange(tk)
    ntri = jnp.asarray(np.where(j[:, None] > j[None, :], -1.0, 0.0), dtype=BF16)
    nbytes = 2 * (2 * seq * width * 2 + 2 * tq * width * 2) + tk * tk * 2 + 8 * g * tq * tk * 4
    return pl.pallas_call(
        _sb_body,
        grid=(bsz, SB_HEADS // g, seq // tq),
        in_specs=[pl.BlockSpec((1, tq, width), lambda b, hg, qi: (b, qi, hg)),
                  pl.BlockSpec((1, seq, width), lambda b, hg, qi: (b, 0, hg)),
                  pl.BlockSpec((1, seq, width), lambda b, hg, qi: (b, 0, hg)),
                  _resident(ntri.shape)],
        out_specs=pl.BlockSpec((1, tq, width), lambda b, hg, qi: (b, qi, hg)),
        out_shape=jax.ShapeDtypeStruct((bsz, seq, SB_HEADS * SB_DIM), BF16),
        scratch_shapes=[pltpu.VMEM((g, tq, LANES), F32), pltpu.VMEM((g, tq, LANES), F32)],
        compiler_params=_params(3, nbytes),
        name="sb_attention",
    )(qb, kb, vb, ntri)


def _odd_proj_body(h_ref, gmix_ref, w_ref, q_ref, k_ref, v_ref):
    u = _rms(h_ref[...], gmix_ref[...]).astype(BF16)
    n = C_HEADS * C_DIM
    qkv = _dot(u, w_ref[...])
    q_ref[...] = (qkv[:, :n] * (C_DIM ** -0.5 * LOG2E)).astype(BF16)
    k_ref[...] = qkv[:, n:2 * n].astype(BF16)
    v_ref[...] = qkv[:, 2 * n:].astype(BF16)


def _odd_proj(h, gmix, w_qkv):
    t, d = h.shape
    tm = ROW_TILE
    n = C_HEADS * C_DIM
    row = lambda m: pl.BlockSpec((tm, m), lambda i: (i, 0))
    nbytes = w_qkv.size * 2 + 2 * tm * d * 4 + 2 * 3 * tm * n * 2 + 2 * tm * 3 * n * 4
    return pl.pallas_call(
        _odd_proj_body,
        grid=(t // tm,),
        in_specs=[row(d), _resident(gmix.shape), _resident(w_qkv.shape)],
        out_specs=[row(n)] * 3,
        out_shape=[jax.ShapeDtypeStruct((t, n), BF16)] * 3,
        compiler_params=_params(1, nbytes),
        name="odd_proj",
    )(h, gmix, w_qkv)


def _band_body(q_ref, k_ref, v_ref, bias_ref, o_ref):
    tq = ATT_TILE

    def group(q0, k0, n_keys, bias_off):
        heads = range(BAND_HEADS_PER_STEP)
        pair_slots = [slice(p * LANES, (p + 1) * LANES) for p in range(BAND_HEADS_PER_STEP // 2)]
        scores = []
        for hh in heads:
            q_pair = q_ref[0, pl.ds(q0, tq), pair_slots[hh // 2]]
            q = jnp.where(_head_select((tq, LANES), hh % 2), q_pair, jnp.zeros_like(q_pair))
            scores.append(_dot_nt(q, k_ref[0, pl.ds(k0, n_keys), pair_slots[hh // 2]]))
        probs = []
        for hh in heads:
            s = scores[hh] + bias_ref[hh, :, bias_off:bias_off + n_keys]
            probs.append(jnp.exp2(s - jnp.max(s, axis=-1, keepdims=True)).astype(BF16))
        outs = []
        for hh in heads:
            v_pair = v_ref[0, pl.ds(k0, n_keys), pair_slots[hh // 2]]
            vals = jnp.where(_head_select((n_keys, LANES), hh % 2), v_pair, jnp.ones_like(v_pair))
            pv = _dot(probs[hh], vals)
            row_sum_lane = (1 - hh % 2) * C_DIM
            outs.append(pv / pv[:, row_sum_lane:row_sum_lane + 1])
        for p in range(BAND_HEADS_PER_STEP // 2):
            packed = jnp.where(_head_select((tq, LANES), 0), outs[2 * p], outs[2 * p + 1])
            o_ref[0, pl.ds(q0, tq), pair_slots[p]] = packed.astype(BF16)

    lead = LEFT_CHUNKS // BAND_GROUP
    for g in range(lead):
        group(g * tq, 0, (g + 1) * tq, LEFT_CHUNKS * CHUNK - g * tq)

    def body(i, _):
        for half in range(2):
            q0 = pl.multiple_of((lead + 2 * i + half) * tq, tq)
            group(q0, pl.multiple_of(q0 - LEFT_CHUNKS * CHUNK, tq), BAND_KEYS, 0)
        return 0

    n_groups = q_ref.shape[1] // tq - lead
    assert n_groups % 2 == 0
    lax.fori_loop(0, n_groups // 2, body, 0)


def _band_bias_body(rev_ref, o_ref):
    tq, n = ATT_TILE, rev_ref.shape[-1]
    rolled = pltpu.roll(jnp.broadcast_to(rev_ref[0], (tq, n)), 1, 1, stride=1, stride_axis=0)
    q_chunk = lax.broadcasted_iota(jnp.int32, (tq, BAND_KEYS), 0) // CHUNK
    k_chunk = lax.broadcasted_iota(jnp.int32, (tq, BAND_KEYS), 1) // CHUNK
    in_band = (k_chunk >= q_chunk) & (k_chunk <= q_chunk + LEFT_CHUNKS)
    o_ref[0] = jnp.where(in_band, rolled[:, tq:tq + BAND_KEYS] * LOG2E, -jnp.inf)


def _band_bias_table(rel_bias):
    tq, n_heads = ATT_TILE, rel_bias.shape[0]
    shift = LEFT_CHUNKS * CHUNK
    assert shift - (BAND_KEYS - 1) >= -REL_CLIP and shift + tq - 1 >= REL_CLIP
    n_rel = tq - 1 + BAND_KEYS
    n_clipped = n_rel - (tq - 1 + REL_CLIP + 1)
    by_rel = jnp.concatenate(
        [rel_bias[:, REL_CLIP - (tq - 1):], jnp.broadcast_to(rel_bias[:, -1:], (n_heads, n_clipped))], axis=1)
    rev = jnp.pad(by_rel[:, ::-1], ((0, 0), (0, 1)))
    assert rev.shape[1] == tq + BAND_KEYS and shift + tq - 1 == n_rel - tq
    return pl.pallas_call(
        _band_bias_body,
        grid=(n_heads,),
        in_specs=[pl.BlockSpec((1, 1, rev.shape[1]), lambda h: (h, 0, 0))],
        out_specs=pl.BlockSpec((1, tq, BAND_KEYS), lambda h: (h, 0, 0)),
        out_shape=jax.ShapeDtypeStruct((n_heads, tq, BAND_KEYS), F32),
        compiler_params=_params(1, 8 * tq * rev.shape[1] * 4),
        name="band_bias",
    )(rev.reshape(n_heads, 1, rev.shape[1]))


def _band_attention(q, k, v, bias_table):
    bsz, seq, _ = q.shape
    g = BAND_HEADS_PER_STEP
    tq = ATT_TILE
    width = g * C_DIM
    seq_spec = pl.BlockSpec((1, seq, width), lambda b, hg: (b, 0, hg))
    nbytes = 2 * (4 * seq * width * 2 + g * tq * BAND_KEYS * 4) + 4 * g * tq * BAND_KEYS * 4
    return pl.pallas_call(
        _band_body,
        grid=(bsz, C_HEADS // g),
        in_specs=[seq_spec, seq_spec, seq_spec,
                  pl.BlockSpec((g, tq, BAND_KEYS), lambda b, hg: (hg, 0, 0))],
        out_specs=seq_spec,
        out_shape=jax.ShapeDtypeStruct((bsz, seq, C_HEADS * C_DIM), BF16),
        compiler_params=_params(2, nbytes),
        name="band_attention",
    )(q, k, v, bias_table)


def _mix_ffn_body(*refs, n_mix, final_norm):
    h_ref, gffn_ref = refs[0], refs[1]
    mix_refs = refs[2:2 + n_mix]
    wo_ref, wg_ref, wu_ref, wd_ref, gfin_ref, o_ref, acc_ref = refs[2 + n_mix:]
    mixed = jnp.concatenate([r[...] for r in mix_refs], axis=1) if n_mix > 1 else mix_refs[0][...]
    h = h_ref[...] + _dot(mixed, wo_ref[...])
    u = _rms(h, gffn_ref[...]).astype(BF16)
    d_ff = wg_ref.shape[1]
    acc_ref[...] = h
    for f0 in range(0, d_ff, FFN_COL_TILE):
        cols = slice(f0, f0 + FFN_COL_TILE)
        gate = _dot(u, wg_ref[:, cols])
        act = (gate * jax.nn.sigmoid(gate) * _dot(u, wu_ref[:, cols])).astype(BF16)
        acc_ref[...] += _dot(act, wd_ref[cols, :])
    out = acc_ref[...]
    o_ref[...] = _rms(out, gfin_ref[...]) if final_norm else out


def _mix_ffn(h, g_ffn, mix_outs, w_out, wg, wu, wd, g_final, final_norm):
    t, d = h.shape
    tm = FFN_ROW_TILE
    d_ff = wg.shape[1]
    assert d_ff % FFN_COL_TILE == 0
    row = lambda n: pl.BlockSpec((tm, n), lambda i: (i, 0))
    nbytes = ((3 * d * d_ff + w_out.size) * 2 + 2 * tm * d * (4 + 4) + 2 * tm * w_out.shape[0] * 2
              + 3 * tm * d * 4 + 4 * tm * FFN_COL_TILE * 4)
    return pl.pallas_call(
        functools.partial(_mix_ffn_body, n_mix=len(mix_outs), final_norm=final_norm),
        grid=(t // tm,),
        in_specs=[row(d), _resident(g_ffn.shape)] + [row(o.shape[1]) for o in mix_outs]
                 + [_resident(w_out.shape), _resident(wg.shape), _resident(wu.shape), _resident(wd.shape),
                    _resident(g_final.shape)],
        out_specs=row(d),
        out_shape=jax.ShapeDtypeStruct((t, d), F32),
        scratch_shapes=[pltpu.VMEM((tm, d), F32)],
        compiler_params=_params(1, nbytes),
        name="mix_ffn",
    )(h, g_ffn, *mix_outs, w_out, wg, wu, wd, g_final)


def _cast_body(*refs):
    n = len(refs) // 2
    for src, dst in zip(refs[:n], refs[n:]):
        dst[...] = src[0].astype(BF16)


def _cast_weights(stacks, indices):
    steps = WEIGHT_CAST_STEPS
    blocks = [(w.shape[1] // steps, w.shape[2]) for w in stacks]
    assert all(w.shape[1] % (steps * 16) == 0 for w in stacks)
    nbytes = 2 * sum(r * c * 6 for r, c in blocks)
    return pl.pallas_call(
        _cast_body,
        grid=(steps,),
        in_specs=[pl.BlockSpec((1, r, c), lambda i, l=l: (l, i, 0)) for (r, c), l in zip(blocks, indices)],
        out_specs=[pl.BlockSpec((r, c), lambda i: (i, 0)) for r, c in blocks],
        out_shape=[jax.ShapeDtypeStruct(w.shape[1:], BF16) for w in stacks],
        compiler_params=_params(1, nbytes),
        name="cast_weights",
    )(*stacks)


def _head_slots(w, n_heads, per_head, start, keep):
    k = w.shape[0]
    w = w.reshape(k, n_heads, per_head)[:, :, start:start + keep]
    return jnp.pad(w, ((0, 0), (0, 0), (0, LANES - keep))).reshape(k, n_heads * LANES)


def _rope_slot_tables(seq, scale):
    half = MLA_ROPE // 2
    pos = np.arange(seq, dtype=np.float64)
    inv_freq = ROPE_THETA ** (-np.arange(0, MLA_ROPE, 2, dtype=np.float64) / MLA_ROPE)
    ang = pos[:, None] * inv_freq[None, :]
    cos, sin = np.cos(ang), np.sin(ang)
    zeros = np.zeros((seq, half))
    pad = np.zeros((seq, LANES - MLA_NOPE - MLA_ROPE))
    ones = np.ones((seq, MLA_NOPE))
    direct = np.concatenate([ones, cos, cos, pad], axis=1)
    from_lower = np.concatenate([0 * ones, zeros, sin, pad], axis=1)
    from_upper = np.concatenate([0 * ones, -sin, zeros, pad], axis=1)
    return jnp.asarray((np.stack([direct, from_lower, from_upper]) * scale).astype(np.float32))


def kernel(x, ev_w_in, ev_g_cq, ev_w_uq, ev_g_ckv, ev_w_ukv, ev_w_out, od_w_qkv, od_rel_bias, od_w_out,
           g_mix, g_ffn, w_gate, w_up, w_down, g_final):
    bsz, seq, d = x.shape
    depth = g_mix.shape[0]
    t = bsz * seq
    h = x.reshape(t, d)
    g_fin = g_final.reshape(1, d)
    rope_q = _rope_slot_tables(seq, (MLA_NOPE + MLA_ROPE) ** -0.5 * LOG2E)
    rope_k = _rope_slot_tables(seq, 1.0)

    for layer in range(depth):
        i = layer // 2
        gm = g_mix[layer].reshape(1, d)
        if layer % 2 == 0:
            w_in = ev_w_in[i]
            o1, o2, o3 = Q_LORA, Q_LORA + KV_LORA, Q_LORA + KV_LORA + MLA_ROPE
            wcq = w_in[:, :o1].astype(BF16)
            wckv = w_in[:, o1:o2].astype(BF16)
            wkr = jnp.pad(w_in[:, o2:o3], ((0, 0), (MLA_NOPE, LANES - MLA_NOPE - MLA_ROPE))).astype(BF16)
            wb = w_in[:, o3:].astype(BF16)
            wuq = _head_slots(ev_w_uq[i], MLA_HEADS, MLA_NOPE + MLA_ROPE, 0, MLA_NOPE + MLA_ROPE).astype(BF16)
            qa, ka, va, qb, kb, vb = _even_proj(
                h, gm, wcq, wckv, wkr, wb, ev_g_cq[i].reshape(1, -1), wuq, ev_g_ckv[i].reshape(1, -1),
                ev_w_ukv[i].astype(BF16), rope_q, rope_k, seq)
            r3 = lambda a: a.reshape(bsz, seq, a.shape[1])
            o_a = _mla_attention(r3(qa), r3(ka), r3(va)).reshape(t, -1)
            o_b = _sb_attention(r3(qb), r3(kb), r3(vb)).reshape(t, -1)
            mix_outs, w_out_stack = [o_a, o_b], ev_w_out
        else:
            (w_qkv,) = _cast_weights([od_w_qkv], [i])
            q, k, v = _odd_proj(h, gm, w_qkv)
            r3 = lambda a: a.reshape(bsz, seq, a.shape[1])
            o_c = _band_attention(r3(q), r3(k), r3(v), _band_bias_table(od_rel_bias[i])).reshape(t, -1)
            mix_outs, w_out_stack = [o_c], od_w_out
        w_out, wg, wu, wd = _cast_weights([w_out_stack, w_gate, w_up, w_down], [i, layer, layer, layer])
        h = _mix_ffn(h, g_ffn[layer].reshape(1, d), mix_outs, w_out, wg, wu, wd, g_fin,
                     final_norm=(layer == depth - 1))
    return h.reshape(bsz, seq, d)
```

```python
import functools

import jax
import jax.numpy as jnp
import numpy as np
from jax import lax
from jax.experimental import pallas as pl
from jax.experimental.pallas import tpu as pltpu

F32 = jnp.float32
BF16 = jnp.bfloat16

CHUNK = 64
MLA_HEADS = 8
MLA_NOPE = 64
MLA_ROPE = 32
MLA_V = 64
Q_LORA = 384
KV_LORA = 256
ROPE_THETA = 10000.0
SB_HEADS = 8
SB_DIM = 64
C_HEADS = 16
C_DIM = 64
LEFT_CHUNKS = 8
REL_CLIP = 256
RMS_EPS = 1e-6

LANES = 128
V7X_VMEM_BYTES = 64 * 1024 * 1024

ROW_TILE = 512
FFN_ROW_TILE = 1024
FFN_COL_TILE = 256
Q_TILE = 512
K_TILE = 256
SB_HEADS_PER_STEP = 8
SB_HEAD_ROWS = 192
MLA_K_TILE = 512
MLA_DIAG_TILE = 256
MLA_HEADS_PER_STEP = 8
BAND_HEADS_PER_STEP = 8
ATT_TILE = 256
BAND_GROUP = ATT_TILE // CHUNK
BAND_KEYS = (LEFT_CHUNKS + BAND_GROUP) * CHUNK
WEIGHT_CAST_STEPS = 8

LOG2E = 1.4426950408889634
EXP_UNDERFLOW = -104.0


def _vmem_limit(nbytes):
    return int(min(max(2 * nbytes, 32 * 1024 * 1024), V7X_VMEM_BYTES - 8 * 1024 * 1024))


def _params(n_axes, nbytes):
    return pltpu.CompilerParams(
        dimension_semantics=("arbitrary",) * n_axes,
        vmem_limit_bytes=_vmem_limit(nbytes),
    )


def _dot(a, b):
    return jnp.dot(a, b, preferred_element_type=F32)


def _dot_nt(a, b):
    return lax.dot_general(a, b, (((1,), (1,)), ((), ())), preferred_element_type=F32)


def _rms(x, g):
    return x * lax.rsqrt(jnp.mean(x * x, axis=-1, keepdims=True) + RMS_EPS) * g


def _resident(shape):
    nd = len(shape)
    return pl.BlockSpec(shape, lambda *_: (0,) * nd, pipeline_mode=pl.Buffered(1))


def _rope(x, tab_ref):
    return (x * tab_ref[0]
            + pltpu.roll(x, MLA_ROPE // 2, 1) * tab_ref[1]
            + pltpu.roll(x, LANES - MLA_ROPE // 2, 1) * tab_ref[2])


def _even_proj_body(h_ref, gmix_ref, win_ref, gcq_ref, wuq_ref, gckv_ref, wukv_ref, ropeq_ref, ropek_ref,
                    qa_ref, ka_ref, va_ref, qb_ref, kb_ref, vb_ref):
    u = _rms(h_ref[...], gmix_ref[...]).astype(BF16)
    nb = SB_HEADS * SB_DIM
    o_kv, o_kr, o_b = Q_LORA, Q_LORA + KV_LORA, Q_LORA + KV_LORA + LANES
    cq = _dot(u, win_ref[:, :o_kv])
    ckv = _dot(u, win_ref[:, o_kv:o_kr])
    qb_ref[...] = (_dot(u, win_ref[:, o_b:o_b + nb]) * (SB_DIM ** -0.5)).astype(BF16)
    cqn = _rms(cq, gcq_ref[...]).astype(BF16)
    ckvn = _rms(ckv, gckv_ref[...]).astype(BF16)
    k_rope = _rope(_dot(u, win_ref[:, o_kr:o_b]), ropek_ref)
    qa = _dot(cqn, wuq_ref[...])
    kv = _dot(ckvn, wukv_ref[...])
    lane = lax.broadcasted_iota(jnp.int32, k_rope.shape, 1)
    v_tail = jnp.where(lane == MLA_V, 1.0, 0.0)
    for hh in range(MLA_HEADS):
        sl = slice(hh * LANES, (hh + 1) * LANES)
        qa_ref[:, sl] = _rope(qa[:, sl], ropeq_ref).astype(BF16)
        ka_ref[:, sl] = jnp.where(lane < MLA_NOPE, kv[:, sl], k_rope).astype(BF16)
        va_ref[:, sl] = jnp.where(lane < MLA_V, pltpu.roll(kv[:, sl], LANES - MLA_NOPE, 1), v_tail).astype(BF16)

    kvb = _dot(u, win_ref[:, o_b + nb:])
    kb_ref[...] = kvb[:, :nb].astype(BF16)
    vb_ref[...] = kvb[:, nb:].astype(BF16)


def _even_proj(h, gmix, win, gcq, wuq, gckv, wukv, ropeq, ropek, seq):
    t, d = h.shape
    tm = ROW_TILE
    per_seq = seq // tm
    assert MLA_NOPE + MLA_V == LANES and wukv.shape[1] == MLA_HEADS * LANES
    assert Q_LORA % LANES == 0 and KV_LORA % LANES == 0
    row = lambda n: pl.BlockSpec((tm, n), lambda i: (i, 0))
    rope_spec = pl.BlockSpec((3, tm, LANES), lambda i: (0, i % per_seq, 0))
    weights = (win, gcq, wuq, gckv, wukv)
    out_widths = (MLA_HEADS * LANES,) * 3 + (SB_HEADS * SB_DIM,) * 3
    nbytes = (sum(w.size * w.dtype.itemsize for w in weights)
              + 2 * tm * d * 4 + 2 * tm * sum(out_widths) * 2 + tm * 8192 * 4)
    return pl.pallas_call(
        _even_proj_body,
        grid=(t // tm,),
        in_specs=[row(d), _resident(gmix.shape), _resident(win.shape), _resident(gcq.shape), _resident(wuq.shape),
                  _resident(gckv.shape), _resident(wukv.shape), rope_spec, rope_spec],
        out_specs=[row(n) for n in out_widths],
        out_shape=[jax.ShapeDtypeStruct((t, n), BF16) for n in out_widths],
        compiler_params=_params(1, nbytes),
        name="even_proj",
    )(h, gmix, win, gcq, wuq, gckv, wukv, ropeq, ropek)


def _head_select(shape, head_in_pair):
    lane = lax.broadcasted_iota(jnp.int32, shape, 1)
    return (lane < SB_DIM) if head_in_pair == 0 else (lane >= SB_DIM)


def _mla_body(q_ref, k_ref, v_ref, o_ref, m_ref, acc_ref):
    tq, tk, n_heads = Q_TILE, MLA_K_TILE, MLA_HEADS_PER_STEP
    assert tq == tk
    dk = MLA_DIAG_TILE
    qi = pl.program_id(2)
    row_chunk = lax.broadcasted_iota(jnp.int32, (tq, dk), 0) // CHUNK
    col_chunk = lax.broadcasted_iota(jnp.int32, (tq, dk), 1) // CHUNK
    slots = [slice(hh * LANES, (hh + 1) * LANES) for hh in range(n_heads)]
    qs = [q_ref[0, :, sl] for sl in slots]

    def tile(k0, n_keys, visible, first_row=0, first_visit=False):
        rows = pl.ds(pl.multiple_of(k0, n_keys), n_keys)
        heads = range(n_heads)
        scores = [_dot_nt(qs[hh][first_row:], k_ref[0, rows, slots[hh]]) for hh in heads]
        probs = []
        for hh in heads:
            s = scores[hh] if visible is None else jnp.where(visible, scores[hh], -jnp.inf)
            m_new = jnp.max(s, axis=-1, keepdims=True)
            if first_visit:
                m_new = jnp.broadcast_to(m_new, (tq - first_row, LANES))
            else:
                m = m_ref[hh, first_row:, :]
                m_new = jnp.maximum(m, m_new)
                acc_ref[hh, first_row:, :] *= jnp.exp2(m - m_new)
            p = [jnp.exp2(s[:, j:j + LANES] - m_new) for j in range(0, n_keys, LANES)]
            probs.append(jnp.concatenate(p, axis=1).astype(BF16))
            m_ref[hh, first_row:, :] = m_new
        for hh in heads:
            pv = _dot(probs[hh], v_ref[0, rows, slots[hh]])
            if first_visit:
                acc_ref[hh, first_row:, :] = pv
            else:
                acc_ref[hh, first_row:, :] += pv

    for d in range(tq // dk):
        first = d * dk
        tile(qi * tq + first, dk, (col_chunk <= row_chunk)[:tq - first], first_row=first, first_visit=(d == 0))

    def full_tile(kb, _):
        tile(kb * tk, tk, None)
        return 0

    lax.fori_loop(0, qi, full_tile, 0)
    outs = [acc_ref[hh] / acc_ref[hh, :, MLA_V:MLA_V + 1] for hh in range(n_heads)]
    lane = lax.broadcasted_iota(jnp.int32, (tq, LANES), 1)
    for p in range(n_heads // 2):
        packed = jnp.where(lane < MLA_V, outs[2 * p], pltpu.roll(outs[2 * p + 1], MLA_V, 1))
        o_ref[0, :, p * LANES:(p + 1) * LANES] = packed.astype(BF16)


def _mla_attention(qa, ka, va):
    bsz, seq, _ = qa.shape
    tq, g = Q_TILE, MLA_HEADS_PER_STEP
    nbytes = 2 * (2 * seq * g * LANES * 2 + tq * g * LANES * 3) + 8 * g * tq * MLA_K_TILE * 4
    return pl.pallas_call(
        _mla_body,
        grid=(bsz, MLA_HEADS // g, seq // tq),
        in_specs=[pl.BlockSpec((1, tq, g * LANES), lambda b, hg, qi: (b, qi, hg)),
                  pl.BlockSpec((1, seq, g * LANES), lambda b, hg, qi: (b, 0, hg)),
                  pl.BlockSpec((1, seq, g * LANES), lambda b, hg, qi: (b, 0, hg))],
        out_specs=pl.BlockSpec((1, tq, g * MLA_V), lambda b, hg, qi: (b, qi, hg)),
        out_shape=jax.ShapeDtypeStruct((bsz, seq, MLA_HEADS * MLA_V), BF16),
        scratch_shapes=[pltpu.VMEM((g, tq, LANES), F32), pltpu.VMEM((g, tq, LANES), F32)],
        compiler_params=_params(3, nbytes),
        name="mla_attention",
    )(qa, ka, va)


def _sb_body(q_ref, k_ref, v_ref, ntri_ref, o_ref, later_ref, acc_ref):
    tq, tk, n_heads = Q_TILE, K_TILE, SB_HEADS_PER_STEP
    qi = pl.program_id(2)
    per = tq // tk
    row = lax.broadcasted_iota(jnp.int32, (tq, tk), 0)
    col = lax.broadcasted_iota(jnp.int32, (tq, tk), 1)
    pair_slots = [slice(p * LANES, (p + 1) * LANES) for p in range(n_heads // 2)]
    qs = []
    for hh in range(n_heads):
        q_pair = q_ref[0, :, pair_slots[hh // 2]]
        qs.append(jnp.where(_head_select((tq, LANES), hh % 2), q_pair, jnp.zeros_like(q_pair)))

    def tile(kb, before, first_row=0, last_row=Q_TILE):
        rows = pl.ds(pl.multiple_of(kb * tk, tk), tk)
        heads = range(n_heads)
        zs = [_dot_nt(qs[hh][first_row:last_row], k_ref[0, rows, pair_slots[hh // 2]]) for hh in heads]
        softplus = []
        for hh in heads:
            sp = jnp.maximum(zs[hh], 0.0) + jnp.log(1.0 + jnp.exp2(jnp.abs(zs[hh]) * -LOG2E))
            softplus.append(sp if before is None else jnp.where(before, sp, 0.0))
        cums = []
        for hh in heads:
            local = _dot(softplus[hh].astype(BF16), ntri_ref[...])
            later = later_ref[hh, first_row:last_row, :]
            cums.append(jnp.concatenate([local[:, j:j + LANES] + later for j in range(0, tk, LANES)], axis=1))
        weights, largest = [], None
        for hh in heads:
            w = jnp.exp((zs[hh] - softplus[hh]) + cums[hh])
            if before is not None:
                w = jnp.where(before, w, 0.0)
            weights.append(w.astype(BF16))
            later = cums[hh][:, 0:1] - softplus[hh][:, 0:1]
            later_ref[hh, first_row:last_row, :] = jnp.broadcast_to(later, (last_row - first_row, LANES))
            largest = later if largest is None else jnp.maximum(largest, later)
        for hh in heads:
            acc_ref[hh, first_row:last_row, :] += _dot(weights[hh], v_ref[0, rows, pair_slots[hh // 2]])
        if first_row:
            return None
        rest = jnp.max(largest[SB_HEAD_ROWS:]) if last_row > SB_HEAD_ROWS else None
        return jnp.max(largest[:SB_HEAD_ROWS]), rest

    later_ref[...] = jnp.zeros(later_ref.shape, F32)
    acc_ref[...] = jnp.zeros(acc_ref.shape, F32)
    for d in reversed(range(per)):
        first = d * tk
        largest_sums = tile(per * qi + d, (col < row)[:tq - first], first_row=first)

    n_earlier = per * qi

    def earlier_tile(state):
        i, head_sum, rest_sum = state
        kb = n_earlier - 1 - i
        head_sum, rest_sum = lax.cond(
            rest_sum <= EXP_UNDERFLOW,
            lambda: (tile(kb, None, last_row=SB_HEAD_ROWS)[0], rest_sum),
            lambda: tile(kb, None))
        return i + 1, head_sum, rest_sum

    lax.while_loop(
        lambda state: jnp.logical_and(state[0] < n_earlier, jnp.maximum(state[1], state[2]) > EXP_UNDERFLOW),
        earlier_tile, (jnp.int32(0),) + largest_sums)
    for p in range(n_heads // 2):
        packed = jnp.where(_head_select((tq, LANES), 0), acc_ref[2 * p], acc_ref[2 * p + 1])
        o_ref[0, :, pair_slots[p]] = packed.astype(BF16)


def _sb_attention(qb, kb, vb):
    bsz, seq, _ = qb.shape
    tq, tk, g = Q_TILE, K_TILE, SB_HEADS_PER_STEP
    width = g * SB_DIM
    j = np.arange(tk)
    ntri = jnp.asarray(np.where(j[:, None] > j[None, :], -1.0, 0.0), dtype=BF16)
    nbytes = 2 * (2 * seq * width * 2 + 2 * tq * width * 2) + tk * tk * 2 + 8 * g * tq * tk * 4
    return pl.pallas_call(
        _sb_body,
        grid=(bsz, SB_HEADS // g, seq // tq),
        in_specs=[pl.BlockSpec((1, tq, width), lambda b, hg, qi: (b, qi, hg)),
                  pl.BlockSpec((1, seq, width), lambda b, hg, qi: (b, 0, hg)),
                  pl.BlockSpec((1, seq, width), lambda b, hg, qi: (b, 0, hg)),
                  _resident(ntri.shape)],
        out_specs=pl.BlockSpec((1, tq, width), lambda b, hg, qi: (b, qi, hg)),
        out_shape=jax.ShapeDtypeStruct((bsz, seq, SB_HEADS * SB_DIM), BF16),
        scratch_shapes=[pltpu.VMEM((g, tq, LANES), F32), pltpu.VMEM((g, tq, LANES), F32)],
        compiler_params=_params(3, nbytes),
        name="sb_attention",
    )(qb, kb, vb, ntri)


def _odd_proj_body(h_ref, gmix_ref, w_ref, q_ref, k_ref, v_ref):
    u = _rms(h_ref[...], gmix_ref[...]).astype(BF16)
    n = C_HEADS * C_DIM
    qkv = _dot(u, w_ref[...])
    q_ref[...] = (qkv[:, :n] * (C_DIM ** -0.5 * LOG2E)).astype(BF16)
    k_ref[...] = qkv[:, n:2 * n].astype(BF16)
    v_ref[...] = qkv[:, 2 * n:].astype(BF16)


def _odd_proj(h, gmix, w_qkv):
    t, d = h.shape
    tm = ROW_TILE
    n = C_HEADS * C_DIM
    row = lambda m: pl.BlockSpec((tm, m), lambda i: (i, 0))
    nbytes = w_qkv.size * 2 + 2 * tm * d * 4 + 2 * 3 * tm * n * 2 + 2 * tm * 3 * n * 4
    return pl.pallas_call(
        _odd_proj_body,
        grid=(t // tm,),
        in_specs=[row(d), _resident(gmix.shape), _resident(w_qkv.shape)],
        out_specs=[row(n)] * 3,
        out_shape=[jax.ShapeDtypeStruct((t, n), BF16)] * 3,
        compiler_params=_params(1, nbytes),
        name="odd_proj",
    )(h, gmix, w_qkv)


def _band_body(q_ref, k_ref, v_ref, bias_ref, o_ref):
    tq = ATT_TILE

    def group(q0, k0, n_keys, bias_off):
        heads = range(BAND_HEADS_PER_STEP)
        pair_slots = [slice(p * LANES, (p + 1) * LANES) for p in range(BAND_HEADS_PER_STEP // 2)]
        scores = []
        for hh in heads:
            q_pair = q_ref[0, pl.ds(q0, tq), pair_slots[hh // 2]]
            q = jnp.where(_head_select((tq, LANES), hh % 2), q_pair, jnp.zeros_like(q_pair))
            scores.append(_dot_nt(q, k_ref[0, pl.ds(k0, n_keys), pair_slots[hh // 2]]))
        probs = []
        for hh in heads:
            s = scores[hh] + bias_ref[hh, :, bias_off:bias_off + n_keys]
            probs.append(jnp.exp2(s - jnp.max(s, axis=-1, keepdims=True)).astype(BF16))
        outs = []
        for hh in heads:
            v_pair = v_ref[0, pl.ds(k0, n_keys), pair_slots[hh // 2]]
            vals = jnp.where(_head_select((n_keys, LANES), hh % 2), v_pair, jnp.ones_like(v_pair))
            pv = _dot(probs[hh], vals)
            row_sum_lane = (1 - hh % 2) * C_DIM
            outs.append(pv / pv[:, row_sum_lane:row_sum_lane + 1])
        for p in range(BAND_HEADS_PER_STEP // 2):
            packed = jnp.where(_head_select((tq, LANES), 0), outs[2 * p], outs[2 * p + 1])
            o_ref[0, pl.ds(q0, tq), pair_slots[p]] = packed.astype(BF16)

    lead = LEFT_CHUNKS // BAND_GROUP
    for g in range(lead):
        group(g * tq, 0, (g + 1) * tq, LEFT_CHUNKS * CHUNK - g * tq)

    def body(i, _):
        for half in range(2):
            q0 = pl.multiple_of((lead + 2 * i + half) * tq, tq)
            group(q0, pl.multiple_of(q0 - LEFT_CHUNKS * CHUNK, tq), BAND_KEYS, 0)
        return 0

    n_groups = q_ref.shape[1] // tq - lead
    assert n_groups % 2 == 0
    lax.fori_loop(0, n_groups // 2, body, 0)


def _band_bias_body(rev_ref, o_ref):
    tq, n = ATT_TILE, rev_ref.shape[-1]
    rolled = pltpu.roll(jnp.broadcast_to(rev_ref[0], (tq, n)), 1, 1, stride=1, stride_axis=0)
    q_chunk = lax.broadcasted_iota(jnp.int32, (tq, BAND_KEYS), 0) // CHUNK
    k_chunk = lax.broadcasted_iota(jnp.int32, (tq, BAND_KEYS), 1) // CHUNK
    in_band = (k_chunk >= q_chunk) & (k_chunk <= q_chunk + LEFT_CHUNKS)
    o_ref[0] = jnp.where(in_band, rolled[:, tq:tq + BAND_KEYS] * LOG2E, -jnp.inf)


def _band_bias_table(rel_bias):
    tq, n_heads = ATT_TILE, rel_bias.shape[0]
    shift = LEFT_CHUNKS * CHUNK
    assert shift - (BAND_KEYS - 1) >= -REL_CLIP and shift + tq - 1 >= REL_CLIP
    n_rel = tq - 1 + BAND_KEYS
    n_clipped = n_rel - (tq - 1 + REL_CLIP + 1)
    by_rel = jnp.concatenate(
        [rel_bias[:, REL_CLIP - (tq - 1):], jnp.broadcast_to(rel_bias[:, -1:], (n_heads, n_clipped))], axis=1)
    rev = jnp.pad(by_rel[:, ::-1], ((0, 0), (0, 1)))
    assert rev.shape[1] == tq + BAND_KEYS and shift + tq - 1 == n_rel - tq
    return pl.pallas_call(
        _band_bias_body,
        grid=(n_heads,),
        in_specs=[pl.BlockSpec((1, 1, rev.shape[1]), lambda h: (h, 0, 0))],
        out_specs=pl.BlockSpec((1, tq, BAND_KEYS), lambda h: (h, 0, 0)),
        out_shape=jax.ShapeDtypeStruct((n_heads, tq, BAND_KEYS), F32),
        compiler_params=_params(1, 8 * tq * rev.shape[1] * 4),
        name="band_bias",
    )(rev.reshape(n_heads, 1, rev.shape[1]))


def _band_attention(q, k, v, bias_table):
    bsz, seq, _ = q.shape
    g = BAND_HEADS_PER_STEP
    tq = ATT_TILE
    width = g * C_DIM
    seq_spec = pl.BlockSpec((1, seq, width), lambda b, hg: (b, 0, hg))
    nbytes = 2 * (4 * seq * width * 2 + g * tq * BAND_KEYS * 4) + 4 * g * tq * BAND_KEYS * 4
    return pl.pallas_call(
        _band_body,
        grid=(bsz, C_HEADS // g),
        in_specs=[seq_spec, seq_spec, seq_spec,
                  pl.BlockSpec((g, tq, BAND_KEYS), lambda b, hg: (hg, 0, 0))],
        out_specs=seq_spec,
        out_shape=jax.ShapeDtypeStruct((bsz, seq, C_HEADS * C_DIM), BF16),
        compiler_params=_params(2, nbytes),
        name="band_attention",
    )(q, k, v, bias_table)


def _mix_ffn_body(*refs, n_mix, final_norm):
    h_ref, gffn_ref = refs[0], refs[1]
    mix_refs = refs[2:2 + n_mix]
    wo_ref, wg_ref, wu_ref, wd_ref, gfin_ref, o_ref, acc_ref = refs[2 + n_mix:]
    mixed = jnp.concatenate([r[...] for r in mix_refs], axis=1) if n_mix > 1 else mix_refs[0][...]
    h = h_ref[...] + _dot(mixed, wo_ref[...])
    u = _rms(h, gffn_ref[...]).astype(BF16)
    d_ff = wg_ref.shape[1]
    acc_ref[...] = h
    for f0 in range(0, d_ff, FFN_COL_TILE):
        cols = slice(f0, f0 + FFN_COL_TILE)
        gate = _dot(u, wg_ref[:, cols])
        act = (gate * jax.nn.sigmoid(gate) * _dot(u, wu_ref[:, cols])).astype(BF16)
        acc_ref[...] += _dot(act, wd_ref[cols, :])
    out = acc_ref[...]
    o_ref[...] = _rms(out, gfin_ref[...]) if final_norm else out


def _mix_ffn(h, g_ffn, mix_outs, w_out, wg, wu, wd, g_final, final_norm):
    t, d = h.shape
    tm = FFN_ROW_TILE
    d_ff = wg.shape[1]
    assert d_ff % FFN_COL_TILE == 0
    row = lambda n: pl.BlockSpec((tm, n), lambda i: (i, 0))
    nbytes = ((3 * d * d_ff + w_out.size) * 2 + 2 * tm * d * (4 + 4) + 2 * tm * w_out.shape[0] * 2
              + 3 * tm * d * 4 + 4 * tm * FFN_COL_TILE * 4)
    return pl.pallas_call(
        functools.partial(_mix_ffn_body, n_mix=len(mix_outs), final_norm=final_norm),
        grid=(t // tm,),
        in_specs=[row(d), _resident(g_ffn.shape)] + [row(o.shape[1]) for o in mix_outs]
                 + [_resident(w_out.shape), _resident(wg.shape), _resident(wu.shape), _resident(wd.shape),
                    _resident(g_final.shape)],
        out_specs=row(d),
        out_shape=jax.ShapeDtypeStruct((t, d), F32),
        scratch_shapes=[pltpu.VMEM((tm, d), F32)],
        compiler_params=_params(1, nbytes),
        name="mix_ffn",
    )(h, g_ffn, *mix_outs, w_out, wg, wu, wd, g_final)


def _cast_body(*refs):
    n = len(refs) // 2
    for src, dst in zip(refs[:n], refs[n:]):
        dst[...] = src[0].astype(BF16)


def _cast_weights(stacks, indices):
    steps = WEIGHT_CAST_STEPS
    blocks = [(w.shape[1] // steps, w.shape[2]) for w in stacks]
    assert all(w.shape[1] % (steps * 16) == 0 for w in stacks)
    nbytes = 2 * sum(r * c * 6 for r, c in blocks)
    return pl.pallas_call(
        _cast_body,
        grid=(steps,),
        in_specs=[pl.BlockSpec((1, r, c), lambda i, l=l: (l, i, 0)) for (r, c), l in zip(blocks, indices)],
        out_specs=[pl.BlockSpec((r, c), lambda i: (i, 0)) for r, c in blocks],
        out_shape=[jax.ShapeDtypeStruct(w.shape[1:], BF16) for w in stacks],
        compiler_params=_params(1, nbytes),
        name="cast_weights",
    )(*stacks)


def _head_slots(w, n_heads, per_head, start, keep):
    k = w.shape[0]
    w = w.reshape(k, n_heads, per_head)[:, :, start:start + keep]
    return jnp.pad(w, ((0, 0), (0, 0), (0, LANES - keep))).reshape(k, n_heads * LANES)


def _rope_slot_tables(seq, scale):
    half = MLA_ROPE // 2
    pos = np.arange(seq, dtype=np.float64)
    inv_freq = ROPE_THETA ** (-np.arange(0, MLA_ROPE, 2, dtype=np.float64) / MLA_ROPE)
    ang = pos[:, None] * inv_freq[None, :]
    cos, sin = np.cos(ang), np.sin(ang)
    zeros = np.zeros((seq, half))
    pad = np.zeros((seq, LANES - MLA_NOPE - MLA_ROPE))
    ones = np.ones((seq, MLA_NOPE))
    direct = np.concatenate([ones, cos, cos, pad], axis=1)
    from_lower = np.concatenate([0 * ones, zeros, sin, pad], axis=1)
    from_upper = np.concatenate([0 * ones, -sin, zeros, pad], axis=1)
    return jnp.asarray((np.stack([direct, from_lower, from_upper]) * scale).astype(np.float32))


def kernel(x, ev_w_in, ev_g_cq, ev_w_uq, ev_g_ckv, ev_w_ukv, ev_w_out, od_w_qkv, od_rel_bias, od_w_out,
           g_mix, g_ffn, w_gate, w_up, w_down, g_final):
    bsz, seq, d = x.shape
    depth = g_mix.shape[0]
    t = bsz * seq
    h = x.reshape(t, d)
    g_fin = g_final.reshape(1, d)
    rope_q = _rope_slot_tables(seq, (MLA_NOPE + MLA_ROPE) ** -0.5 * LOG2E)
    rope_k = _rope_slot_tables(seq, 1.0)

    for layer in range(depth):
        i = layer // 2
        gm = g_mix[layer].reshape(1, d)
        if layer % 2 == 0:
            w_in = ev_w_in[i]
            o2, o3 = Q_LORA + KV_LORA, Q_LORA + KV_LORA + MLA_ROPE
            zeros = lambda n: jnp.zeros((w_in.shape[0], n), w_in.dtype)
            win = jnp.concatenate([w_in[:, :o2], zeros(MLA_NOPE), w_in[:, o2:o3],
                                   zeros(LANES - MLA_NOPE - MLA_ROPE), w_in[:, o3:]], axis=1).astype(BF16)
            wuq = _head_slots(ev_w_uq[i], MLA_HEADS, MLA_NOPE + MLA_ROPE, 0, MLA_NOPE + MLA_ROPE).astype(BF16)
            qa, ka, va, qb, kb, vb = _even_proj(
                h, gm, win, ev_g_cq[i].reshape(1, -1), wuq, ev_g_ckv[i].reshape(1, -1),
                ev_w_ukv[i].astype(BF16), rope_q, rope_k, seq)
            r3 = lambda a: a.reshape(bsz, seq, a.shape[1])
            o_a = _mla_attention(r3(qa), r3(ka), r3(va)).reshape(t, -1)
            o_b = _sb_attention(r3(qb), r3(kb), r3(vb)).reshape(t, -1)
            mix_outs, w_out_stack = [o_a, o_b], ev_w_out
        else:
            (w_qkv,) = _cast_weights([od_w_qkv], [i])
            q, k, v = _odd_proj(h, gm, w_qkv)
            r3 = lambda a: a.reshape(bsz, seq, a.shape[1])
            o_c = _band_attention(r3(q), r3(k), r3(v), _band_bias_table(od_rel_bias[i])).reshape(t, -1)
            mix_outs, w_out_stack = [o_c], od_w_out
        w_out, wg, wu, wd = _cast_weights([w_out_stack, w_gate, w_up, w_down], [i, layer, layer, layer])
        h = _mix_ffn(h, g_ffn[layer].reshape(1, d), mix_outs, w_out, wg, wu, wd, g_fin,
                     final_norm=(layer == depth - 1))
    return h.reshape(bsz, seq, d)
```
